```python
import math
import jax, jax.numpy as jnp
from jax import lax
import numpy as np

D_MODEL = 1024
BATCH = 16
SEQ = 256
DEPTH = 4
DEC_BATCH = 2
DEC_SEQ = 1024
PAST_LEN = 512

GRID_W = 64
FOURIER_WIDTH = 512
FOURIER_GROUPS = 4
FOURIER_GROUP_DIM = FOURIER_WIDTH // FOURIER_GROUPS
CONV_CHANNELS = 512
CONV_TAPS = 31
CONV_PAD = CONV_TAPS // 2
SSM_WIDTH = 512
SSM_GROUP_DIM = 16
SSM_GROUPS = SSM_WIDTH // SSM_GROUP_DIM
SSM_STATE = 64
N_BRANCHES = 3
IN_COLS = FOURIER_WIDTH + 2 * CONV_CHANNELS + SSM_WIDTH + N_BRANCHES * D_MODEL
N_EXPERTS = 32
TOP_K = 4
D_FF = 1024
SWIGLU_LIMIT = 7.0
SWIGLU_ALPHA = 1.702
EPS = 1e-6
N_MOD = 6

kernel_name = "hybrid_fourier_conv_s5_moe_flow_step"

F32 = jnp.float32


def _rmsnorm(x, g):
    xf = x.astype(F32)
    y = xf * lax.rsqrt(jnp.mean(xf * xf, axis=-1, keepdims=True) + EPS)
    return (y * g.astype(F32)).astype(x.dtype)


def _layernorm(x, g, b):
    xf = x.astype(F32)
    mu = jnp.mean(xf, axis=-1, keepdims=True)
    var = jnp.mean(jnp.square(xf - mu), axis=-1, keepdims=True)
    y = (xf - mu) * lax.rsqrt(var + EPS)
    return (y * g.astype(F32) + b.astype(F32)).astype(x.dtype)


def _grid_pos_embed(rows):
    quarter = D_MODEL // 4
    freqs = jnp.exp(-math.log(10000.0) * jnp.arange(quarter, dtype=F32) / quarter)
    r = jnp.repeat(jnp.arange(rows, dtype=F32), GRID_W)
    col = jnp.tile(jnp.arange(GRID_W, dtype=F32), rows)
    ar = r[:, None] * freqs
    ac = col[:, None] * freqs
    return jnp.concatenate([jnp.sin(ar), jnp.cos(ar), jnp.sin(ac), jnp.cos(ac)], axis=-1)


def _adaln(cond, w_mod, b_mod):
    return jax.nn.silu(cond) @ w_mod + b_mod


def _fourier_mix(z):
    bsz, L, _ = z.shape
    zf = z.astype(F32).reshape(bsz, L, FOURIER_GROUPS, FOURIER_GROUP_DIM)
    mixed = jnp.fft.fft2(zf, axes=(1, 3), norm="ortho").real
    return mixed.reshape(bsz, L, FOURIER_WIDTH).astype(z.dtype)


def _conformer_conv(za, zb, w_dw, b_dw, ln_g, ln_b):
    v = za * jax.nn.sigmoid(zb)
    y = lax.conv_general_dilated(
        v, w_dw[:, None, :].astype(v.dtype), window_strides=(1,), padding=[(CONV_PAD, CONV_PAD)],
        dimension_numbers=("NWC", "WIO", "NWC"), feature_group_count=CONV_CHANNELS) + b_dw
    return jax.nn.silu(_layernorm(y, ln_g, ln_b))


def _scan_combine(left, right):
    a_l, b_l = left
    a_r, b_r = right
    return a_l * a_r, a_r * b_l + b_r


def _s5_bidirectional(u, lam_re, lam_im, log_dt, b_re, b_im, c_re, c_im, d, h0):
    bsz, L, _ = u.shape
    ug = u.astype(F32).reshape(bsz, L, SSM_GROUPS, SSM_GROUP_DIM)
    uc = ug.astype(jnp.complex64)
    lam = lax.complex(lam_re.astype(F32), lam_im.astype(F32))
    dt = jnp.exp(log_dt.astype(F32))[..., None]
    lam_bar = jnp.exp(lam * dt)
    b_bar = ((lam_bar - 1.0) / lam)[..., None] * lax.complex(b_re.astype(F32), b_im.astype(F32))
    c_mat = lax.complex(c_re.astype(F32), c_im.astype(F32))
    y = d.astype(F32).reshape(SSM_GROUPS, SSM_GROUP_DIM) * ug
    finals = []
    for direction, rev in ((0, False), (1, True)):
        bu = jnp.einsum("blgh,gph->blgp", uc, b_bar[direction])
        if h0 is not None:
            edge = -1 if rev else 0
            bu = bu.at[:, edge].add(lam_bar[direction] * h0[:, direction])
        a = jnp.broadcast_to(lam_bar[direction], bu.shape)
        _, s = lax.associative_scan(_scan_combine, (a, bu), reverse=rev, axis=1)
        y = y + jnp.einsum("blgp,ghp->blgh", s, c_mat[direction]).real
        if h0 is None:
            finals.append(s[:, 0] if rev else s[:, -1])
    y = y.reshape(bsz, L, SSM_WIDTH).astype(u.dtype)
    if h0 is None:
        return y, jnp.stack(finals, axis=1)
    return y, None


def _moe(h, router_w, router_b, w_gu, b_gu, w_dn, b_dn):
    bsz, L, D = h.shape
    t = h.reshape(-1, D)
    logits = (t @ router_w + router_b).astype(F32)
    top_v, top_i = lax.top_k(logits, TOP_K)
    probs = jax.nn.softmax(top_v, axis=-1)
    combine = jnp.sum(jax.nn.one_hot(top_i, N_EXPERTS, dtype=F32) * probs[..., None], axis=1)

    def expert_step(acc, xs):
        wgu, bgu, wdn, bdn, wt = xs
        gu = t @ wgu + bgu
        gate, up = jnp.split(gu, 2, axis=-1)
        gate = jnp.minimum(gate, SWIGLU_LIMIT)
        up = jnp.clip(up, -SWIGLU_LIMIT, SWIGLU_LIMIT)
        act = (up + 1.0) * (gate * jax.nn.sigmoid(SWIGLU_ALPHA * gate))
        out = act @ wdn + bdn
        return acc + wt[:, None].astype(t.dtype) * out, None

    acc, _ = lax.scan(expert_step, jnp.zeros_like(t), (w_gu, b_gu, w_dn, b_dn, combine.T))
    return acc.reshape(bsz, L, D)


def _layer(x, mod, p, h0):
    sh1, sc1, g1, sh2, sc2, g2 = jnp.split(mod[:, None, :].astype(x.dtype), N_MOD, axis=-1)
    h = _rmsnorm(x, p["norm1_g"]) * (1.0 + sc1) + sh1
    z = h @ p["w_in"] + p["b_in"]
    cuts = [FOURIER_WIDTH, FOURIER_WIDTH + CONV_CHANNELS, FOURIER_WIDTH + 2 * CONV_CHANNELS,
            FOURIER_WIDTH + 2 * CONV_CHANNELS + SSM_WIDTH]
    z_f, z_a, z_b, z_s, z_g = jnp.split(z, cuts, axis=-1)
    br_f = _fourier_mix(z_f) @ p["w_four"]
    br_c = _conformer_conv(z_a, z_b, p["conv_dw"], p["conv_dw_b"], p["conv_ln_g"], p["conv_ln_b"]) @ p["w_conv_out"]
    y_s, finals = _s5_bidirectional(z_s, p["ssm_lam_re"], p["ssm_lam_im"], p["ssm_log_dt"], p["ssm_b_re"],
                                    p["ssm_b_im"], p["ssm_c_re"], p["ssm_c_im"], p["ssm_d"], h0)
    gs = jax.nn.gelu(y_s)
    y_s = gs * jax.nn.sigmoid(gs @ p["w_ssm_glu"] + p["b_ssm_glu"])
    br_s = y_s @ p["w_ssm_out"]
    gate_f, gate_c, gate_s = jnp.split(jax.nn.sigmoid(z_g), N_BRANCHES, axis=-1)
    mixed = (gate_f * br_f + gate_c * br_c + gate_s * br_s) @ p["w_out"]
    x = x + g1 * mixed
    h = _rmsnorm(x, p["norm2_g"]) * (1.0 + sc2) + sh2
    x = x + g2 * _moe(h, p["router_w"], p["router_b"], p["w_gate_up"], p["b_gate_up"], p["w_down"], p["b_down"])
    return x, finals


def setup_inputs(seed: int = 0) -> dict:
    key = jax.random.key(seed)
    ks = iter(jax.random.split(key, 48))

    def nrm(shape, scale):
        return jax.random.normal(next(ks), shape, F32) * scale

    G, P, H = SSM_GROUPS, SSM_STATE, SSM_GROUP_DIM
    lam_im_init = jnp.broadcast_to(math.pi * jnp.arange(P, dtype=F32), (DEPTH, 2, G, P))
    inp = {
        "x_prompt": nrm((BATCH, SEQ, D_MODEL), 1.0),
        "x_sample": nrm((DEC_BATCH, DEC_SEQ, D_MODEL), 1.0),
        "state_ssm_re": nrm((DEC_BATCH, DEPTH, 2, G, P), 0.5),
        "state_ssm_im": nrm((DEC_BATCH, DEPTH, 2, G, P), 0.5),
        "c": nrm((DEC_BATCH, D_MODEL), 1.0),
        "c_ctx": nrm((D_MODEL,), 1.0),
        "w_mod": nrm((DEPTH, D_MODEL, N_MOD * D_MODEL), 0.3 * D_MODEL ** -0.5),
        "b_mod": nrm((DEPTH, N_MOD * D_MODEL), 0.02),
        "norm1_g": 1.0 + nrm((DEPTH, D_MODEL), 0.02),
        "norm2_g": 1.0 + nrm((DEPTH, D_MODEL), 0.02),
        "w_in": nrm((DEPTH, D_MODEL, IN_COLS), D_MODEL ** -0.5),
        "b_in": nrm((DEPTH, IN_COLS), 0.02),
        "w_four": nrm((DEPTH, FOURIER_WIDTH, D_MODEL), FOURIER_WIDTH ** -0.5),
        "conv_dw": nrm((DEPTH, CONV_TAPS, CONV_CHANNELS), CONV_TAPS ** -0.5),
        "conv_dw_b": nrm((DEPTH, CONV_CHANNELS), 0.02),
        "conv_ln_g": 1.0 + nrm((DEPTH, CONV_CHANNELS), 0.02),
        "conv_ln_b": nrm((DEPTH, CONV_CHANNELS), 0.02),
        "w_conv_out": nrm((DEPTH, CONV_CHANNELS, D_MODEL), CONV_CHANNELS ** -0.5),
        "ssm_lam_re": -0.5 + nrm((DEPTH, 2, G, P), 0.01),
        "ssm_lam_im": lam_im_init + nrm((DEPTH, 2, G, P), 0.01),
        "ssm_log_dt": jax.random.uniform(next(ks), (DEPTH, 2, G), F32, math.log(1e-3), math.log(1e-1)),
        "ssm_b_re": nrm((DEPTH, 2, G, P, H), (2.0 * H) ** -0.5),
        "ssm_b_im": nrm((DEPTH, 2, G, P, H), (2.0 * H) ** -0.5),
        "ssm_c_re": nrm((DEPTH, 2, G, H, P), (2.0 * P) ** -0.5),
        "ssm_c_im": nrm((DEPTH, 2, G, H, P), (2.0 * P) ** -0.5),
        "ssm_d": nrm((DEPTH, SSM_WIDTH), 0.5),
        "w_ssm_glu": nrm((DEPTH, SSM_WIDTH, SSM_WIDTH), SSM_WIDTH ** -0.5),
        "b_ssm_glu": nrm((DEPTH, SSM_WIDTH), 0.02),
        "w_ssm_out": nrm((DEPTH, SSM_WIDTH, D_MODEL), SSM_WIDTH ** -0.5),
        "w_out": nrm((DEPTH, D_MODEL, D_MODEL), D_MODEL ** -0.5),
        "router_w": nrm((DEPTH, D_MODEL, N_EXPERTS), D_MODEL ** -0.5),
        "router_b": nrm((DEPTH, N_EXPERTS), 0.01),
        "w_gate_up": nrm((DEPTH, N_EXPERTS, D_MODEL, 2 * D_FF), D_MODEL ** -0.5),
        "b_gate_up": nrm((DEPTH, N_EXPERTS, 2 * D_FF), 0.02),
        "w_down": nrm((DEPTH, N_EXPERTS, D_FF, D_MODEL), D_FF ** -0.5),
        "b_down": nrm((DEPTH, N_EXPERTS, D_MODEL), 0.02),
        "final_norm_g": 1.0 + nrm((D_MODEL,), 0.02),
    }
    return inp


def reference(x_prompt, x_sample, state_ssm_re, state_ssm_im, c, c_ctx, w_mod, b_mod, norm1_g, norm2_g, w_in, b_in,
              w_four, conv_dw, conv_dw_b, conv_ln_g, conv_ln_b, w_conv_out, ssm_lam_re, ssm_lam_im, ssm_log_dt,
              ssm_b_re, ssm_b_im, ssm_c_re, ssm_c_im, ssm_d, w_ssm_glu, b_ssm_glu, w_ssm_out, w_out, router_w,
              router_b, w_gate_up, b_gate_up, w_down, b_down, final_norm_g):
    rows = x_sample.shape[1] // GRID_W
    xs = x_sample + _grid_pos_embed(rows).astype(x_sample.dtype)[None]
    xp = x_prompt
    ctx_finals = []
    for i in range(DEPTH):
        p = dict(norm1_g=norm1_g[i], norm2_g=norm2_g[i], w_in=w_in[i], b_in=b_in[i], w_four=w_four[i],
                 conv_dw=conv_dw[i], conv_dw_b=conv_dw_b[i], conv_ln_g=conv_ln_g[i], conv_ln_b=conv_ln_b[i],
                 w_conv_out=w_conv_out[i], ssm_lam_re=ssm_lam_re[i], ssm_lam_im=ssm_lam_im[i],
                 ssm_log_dt=ssm_log_dt[i], ssm_b_re=ssm_b_re[i], ssm_b_im=ssm_b_im[i], ssm_c_re=ssm_c_re[i],
                 ssm_c_im=ssm_c_im[i], ssm_d=ssm_d[i], w_ssm_glu=w_ssm_glu[i], b_ssm_glu=b_ssm_glu[i],
                 w_ssm_out=w_ssm_out[i], w_out=w_out[i], router_w=router_w[i], router_b=router_b[i],
                 w_gate_up=w_gate_up[i], b_gate_up=b_gate_up[i], w_down=w_down[i], b_down=b_down[i])
        mod_ctx = _adaln(c_ctx[None, :], w_mod[i], b_mod[i])
        xp, fin = _layer(xp, mod_ctx, p, None)
        ctx_finals.append(fin)
        mod_lat = _adaln(c, w_mod[i], b_mod[i])
        h0 = lax.complex(state_ssm_re[:, i].astype(F32), state_ssm_im[:, i].astype(F32))
        xs, _ = _layer(xs, mod_lat, p, h0)
    states = jnp.stack(ctx_finals, axis=1)
    new_state_ssm_re = jnp.real(states).astype(x_prompt.dtype)
    new_state_ssm_im = jnp.imag(states).astype(x_prompt.dtype)
    y_prompt = _rmsnorm(xp, final_norm_g)
    y_sample = _rmsnorm(xs, final_norm_g)
    return (y_prompt, y_sample, new_state_ssm_re, new_state_ssm_im)
```

```python
import functools
import math

import numpy as np
import jax
import jax.numpy as jnp
from jax import lax
from jax.experimental import pallas as pl
from jax.experimental.pallas import tpu as pltpu

F32 = jnp.float32
BF16 = jnp.bfloat16
HI = lax.Precision.HIGHEST

D = 1024
BATCH, SEQ = 16, 256
DEC_BATCH, DEC_SEQ = 2, 1024
DEPTH = 4
N_CTX = BATCH * SEQ
N_LAT = DEC_BATCH * DEC_SEQ
N = N_CTX + N_LAT
GRID_W = 64
FW, FG = 512, 4
FGD = FW // FG
CC, TAPS = 512, 31
PAD = TAPS // 2
SW, SH = 512, 16
SG = SW // SH
SP = 64
IN_COLS = FW + 2 * CC + SW + 3 * D
NE, TOPK, DFF = 32, 4, 1024
LIMIT, ALPHA = 7.0, 1.702
EPS = 1e-6
NMOD = 6

TB = 256
NB = N // TB
CTX_BLOCKS = N_CTX // TB
LAT_BLOCKS_PER_SEQ = DEC_SEQ // TB
CH = 16
NCH = N // CH
NCH_CTX = N_CTX // CH
CPS_CTX = SEQ // CH
CPS_LAT = DEC_SEQ // CH
NPAIR = SG // 2
TM = 256
NT_MAX = (N * TOPK + NE * (TM - 1) + TM - 1) // TM
R_MAX = NT_MAX * TM
SUB = 8
LANES = 128
DC = D // LANES
SCAT = 16
FC = 512
VMEM_DEFAULT = 48 * 1024 * 1024
VMEM_MOE = 62 * 1024 * 1024


def _cparams(vmem=VMEM_DEFAULT, ndim=1):
    return pltpu.CompilerParams(dimension_semantics=("arbitrary",) * ndim, vmem_limit_bytes=vmem)


def _const_spec(shape, layer=None):
    if layer is None:
        zeros = (0,) * len(shape)
        return pl.BlockSpec(shape, lambda *_: zeros)
    zeros = (0,) * len(shape)
    return pl.BlockSpec((None,) + tuple(shape), lambda *_: (layer,) + zeros, pipeline_mode=pl.Buffered(1))


def _mod_row(i):
    return jnp.where(i < CTX_BLOCKS, 0, 1 + (i - CTX_BLOCKS) // LAT_BLOCKS_PER_SEQ)


def _sigmoid(x):
    return 1.0 / (1.0 + jnp.exp(-x))


def _rows_to_std(ref, rows):
    return jnp.concatenate([ref[pl.ds(c, rows, stride=DC), :] for c in range(DC)], axis=1)


def _std_to_rows(ref, val, rows):
    for c in range(DC):
        ref[pl.ds(c, rows, stride=DC), :] = val[:, c * LANES:(c + 1) * LANES]


MOD_COLS = 1536


def _mod_kernel(cond_ref, w_ref, b_ref, o_ref):
    c = cond_ref[...]
    s = c * _sigmoid(c)
    o_ref[...] = jnp.dot(s.astype(BF16), w_ref[...].astype(BF16), preferred_element_type=F32) + b_ref[...]


def _adaln(cond8, w_mod, b_mod):
    nc = NMOD * D // MOD_COLS
    return pl.pallas_call(
        _mod_kernel,
        grid=(DEPTH, nc),
        in_specs=[
            pl.BlockSpec((SUB, D), lambda l, j: (0, 0)),
            pl.BlockSpec((None, D, MOD_COLS), lambda l, j: (l, 0, j)),
            pl.BlockSpec((None, 1, MOD_COLS), lambda l, j: (l, 0, j)),
        ],
        out_specs=pl.BlockSpec((None, SUB, MOD_COLS), lambda l, j: (l, 0, j)),
        out_shape=jax.ShapeDtypeStruct((DEPTH, SUB, NMOD * D), F32),
        compiler_params=_cparams(ndim=2),
        name="adaln_mod",
    )(cond8, w_mod, b_mod.reshape(DEPTH, 1, NMOD * D))


def _inproj_kernel(has_moe, *refs):
    if has_moe:
        (x_ref, moe_ref, modp_ref, mod_ref, g_ref, w_ref, b_ref,
         xo_ref, zf_ref, v_ref, u_ref, gate_ref) = refs
        x = x_ref[...] + modp_ref[:, 5 * D:6 * D] * _rows_to_std(moe_ref, TB)
        xo_ref[...] = x
    else:
        x_ref, mod_ref, g_ref, w_ref, b_ref, zf_ref, v_ref, u_ref, gate_ref = refs
        x = x_ref[...]
    sh1 = mod_ref[:, 0:D]
    sc1 = mod_ref[:, D:2 * D]
    ms = jnp.mean(x * x, axis=-1, keepdims=True)
    h = (x * lax.rsqrt(ms + EPS)) * g_ref[...] * (1.0 + sc1) + sh1
    z = jnp.dot(h.astype(BF16), w_ref[...], preferred_element_type=F32) + b_ref[...]
    o = 0
    zf_ref[...] = z[:, o:o + FW].astype(BF16)
    o += FW
    za = z[:, o:o + CC]
    zb = z[:, o + CC:o + 2 * CC]
    v_ref[...] = (za * _sigmoid(zb)).astype(BF16)
    o += 2 * CC
    u_ref[...] = z[:, o:o + SW].astype(BF16)
    o += SW
    gate_ref[...] = _sigmoid(z[:, o:]).astype(BF16)


def _inproj(layer, x, moe, mod_prev, mod, norm_g, w_in, b_in):
    has_moe = moe is not None
    row = lambda i: (i, 0)
    mod_spec = pl.BlockSpec((None, 1, NMOD * D), lambda i: (_mod_row(i), 0, 0))
    in_specs = [pl.BlockSpec((TB, D), row)]
    args = [x]
    if has_moe:
        in_specs += [pl.BlockSpec((TB * DC, LANES), row), mod_spec]
        args += [moe, mod_prev]
    in_specs += [mod_spec, _const_spec((1, D), layer), _const_spec((D, IN_COLS), layer),
                 _const_spec((1, IN_COLS), layer)]
    args += [mod, norm_g, w_in, b_in]
    out_specs = [pl.BlockSpec((TB, FW), row), pl.BlockSpec((TB, CC), row), pl.BlockSpec((TB, SW), row),
                 pl.BlockSpec((TB, 3 * D), row)]
    out_shape = [jax.ShapeDtypeStruct((N, FW), BF16), jax.ShapeDtypeStruct((N, CC), BF16),
                 jax.ShapeDtypeStruct((N, SW), BF16), jax.ShapeDtypeStruct((N, 3 * D), BF16)]
    if has_moe:
        out_specs = [pl.BlockSpec((TB, D), row)] + out_specs
        out_shape = [jax.ShapeDtypeStruct((N, D), F32)] + out_shape
    outs = pl.pallas_call(
        functools.partial(_inproj_kernel, has_moe),
        grid=(NB,), in_specs=in_specs, out_specs=out_specs, out_shape=out_shape,
        compiler_params=_cparams(), name=f"inproj_{int(has_moe)}",
    )(*args)
    if has_moe:
        return outs
    return [x] + list(outs)


def _fourier_kernel(scale, zf_ref, csbd_ref, csl_ref, wf_ref, o_ref):
    t = jnp.dot(zf_ref[...], csbd_ref[...], preferred_element_type=F32).astype(BF16)
    ts = jnp.concatenate([t[:, :FW], t[:, FW:]], axis=0)
    y = jnp.dot(csl_ref[...], ts, preferred_element_type=F32) * scale
    o_ref[...] = jnp.dot(y.astype(BF16), wf_ref[...], preferred_element_type=F32).astype(BF16)


def _dft_consts(L):
    k = np.arange(L)
    ang = 2.0 * np.pi * ((k[:, None] * k[None, :]) % L) / L
    csl = np.concatenate([np.cos(ang), -np.sin(ang)], axis=1)
    m = np.arange(FGD)
    angc = 2.0 * np.pi * ((m[:, None] * m[None, :]) % FGD) / FGD
    cbd = np.kron(np.eye(FG), np.cos(angc))
    sbd = np.kron(np.eye(FG), np.sin(angc))
    csbd = np.concatenate([cbd, sbd], axis=1)
    return jnp.asarray(csl, F32).astype(BF16), jnp.asarray(csbd, F32).astype(BF16)


def _fourier(layer, zf, w_four, L, first_block, nseq):
    csl, csbd = _dft_consts(L)
    scale = 1.0 / math.sqrt(L * FGD)
    return pl.pallas_call(
        functools.partial(_fourier_kernel, scale),
        grid=(nseq,),
        in_specs=[pl.BlockSpec((L, FW), lambda i: (i + first_block, 0)),
                  _const_spec((FW, 2 * FW)), _const_spec((L, 2 * L)), _const_spec((FW, D), layer)],
        out_specs=pl.BlockSpec((L, D), lambda i: (i, 0)),
        out_shape=jax.ShapeDtypeStruct((nseq * L, D), BF16),
        compiler_params=_cparams(), name=f"fourier_{L}",
    )(zf, csbd, csl, w_four)


HALO = 16
CONV_ROWS = 64


def _conv_kernel(vp_ref, vc_ref, vn_ref, cw_ref, cb_ref, lg_ref, lb_ref, wc_ref, o_ref, pad_ref, act_ref):
    i = pl.program_id(0)
    pos = (i - CTX_BLOCKS) % LAT_BLOCKS_PER_SEQ
    lat = i >= CTX_BLOCKS
    has_prev = jnp.logical_and(lat, pos != 0)
    has_next = jnp.logical_and(lat, pos != LAT_BLOCKS_PER_SEQ - 1)
    prev = vp_ref[TB - HALO:, :].astype(F32)
    nxt = vn_ref[:HALO, :].astype(F32)
    pad_ref[0:HALO, :] = jnp.where(has_prev, prev, 0.0)
    pad_ref[HALO:HALO + TB, :] = vc_ref[...].astype(F32)
    pad_ref[HALO + TB:, :] = jnp.where(has_next, nxt, 0.0)
    for r0 in range(0, TB, CONV_ROWS):
        acc = jnp.broadcast_to(cb_ref[...], (CONV_ROWS, CC))
        for j in range(TAPS):
            s = r0 + HALO - PAD + j
            acc = acc + pad_ref[s:s + CONV_ROWS, :] * cw_ref[j:j + 1, :]
        mu = jnp.mean(acc, axis=-1, keepdims=True)
        xc = acc - mu
        var = jnp.mean(xc * xc, axis=-1, keepdims=True)
        y = xc * lax.rsqrt(var + EPS) * lg_ref[...] + lb_ref[...]
        act_ref[r0:r0 + CONV_ROWS, :] = (y * _sigmoid(y)).astype(BF16)
    o_ref[...] = jnp.dot(act_ref[...], wc_ref[...], preferred_element_type=F32).astype(BF16)


def _conv(layer, v, conv_dw, conv_b, ln_g, ln_b, w_conv_out):
    return pl.pallas_call(
        _conv_kernel,
        grid=(NB,),
        in_specs=[pl.BlockSpec((TB, CC), lambda i: (jnp.maximum(i - 1, 0), 0)),
                  pl.BlockSpec((TB, CC), lambda i: (i, 0)),
                  pl.BlockSpec((TB, CC), lambda i: (jnp.minimum(i + 1, NB - 1), 0)),
                  _const_spec((TAPS, CC), layer), _const_spec((1, CC), layer), _const_spec((1, CC), layer),
                  _const_spec((1, CC), layer), _const_spec((CC, D), layer)],
        out_specs=pl.BlockSpec((TB, D), lambda i: (i, 0)),
        out_shape=jax.ShapeDtypeStruct((N, D), BF16),
        scratch_shapes=[pltpu.VMEM((TB + 2 * HALO, CC), F32), pltpu.VMEM((TB, CC), BF16)],
        compiler_params=_cparams(), name="conv_branch",
    )(v, v, v, conv_dw, conv_b, ln_g, ln_b, w_conv_out)


def _ssm_params(lam_re, lam_im, log_dt, b_re, b_im, c_re, c_im, d):
    dt = jnp.exp(log_dt)[..., None]
    ldr, ldi = lam_re * dt, lam_im * dt
    j = jnp.arange(CH + 1, dtype=F32)
    mag = jnp.exp(ldr[..., None] * j)
    pr, pi = mag * jnp.cos(ldi[..., None] * j), mag * jnp.sin(ldi[..., None] * j)
    nr, ni = pr[..., 1] - 1.0, pi[..., 1]
    den = lam_re * lam_re + lam_im * lam_im
    qr, qi = (nr * lam_re + ni * lam_im) / den, (ni * lam_re - nr * lam_im) / den
    bbr = qr[..., None] * b_re - qi[..., None] * b_im
    bbi = qr[..., None] * b_im + qi[..., None] * b_re
    xr = pr[..., :CH, None] * bbr[..., None, :] - pi[..., :CH, None] * bbi[..., None, :]
    xi = pr[..., :CH, None] * bbi[..., None, :] + pi[..., :CH, None] * bbr[..., None, :]
    kk = (jnp.einsum("dghp,dgpji->dgjhi", c_re, xr, precision=HI)
          - jnp.einsum("dghp,dgpji->dgjhi", c_im, xi, precision=HI))
    ki = np.arange(CH)[:, None]
    ti = np.arange(CH)[None, :]
    lagf, mf = np.clip(ti - ki, 0, CH - 1), (ki <= ti).astype(np.float32)
    lagb, mb = np.clip(ki - ti, 0, CH - 1), (ki >= ti).astype(np.float32)
    mm = kk[0][:, lagf] * mf[None, :, :, None, None] + kk[1][:, lagb] * mb[None, :, :, None, None]
    mm = jnp.transpose(mm, (0, 1, 4, 2, 3))
    eye_t = np.eye(CH, dtype=np.float32)
    eye_h = np.eye(SH, dtype=np.float32)
    dd = d.reshape(SG, SH)
    mm = mm + dd[:, None, :, None, None] * eye_t[None, :, None, :, None] * eye_h[None, None, :, None, :]
    m = mm.reshape(SG, CH * SH, CH * SH)
    wfr = jnp.transpose(xr[0][:, :, ::-1, :], (0, 2, 3, 1)).reshape(SG, CH * SH, SP)
    wfi = jnp.transpose(xi[0][:, :, ::-1, :], (0, 2, 3, 1)).reshape(SG, CH * SH, SP)
    wbr = jnp.transpose(xr[1], (0, 2, 3, 1)).reshape(SG, CH * SH, SP)
    wbi = jnp.transpose(xi[1], (0, 2, 3, 1)).reshape(SG, CH * SH, SP)
    ctr, cti = jnp.transpose(c_re, (0, 1, 3, 2)), jnp.transpose(c_im, (0, 1, 3, 2))
    pfr, pfi = pr[0][..., 1:], pi[0][..., 1:]
    pbr, pbi = pr[1][..., 1:][..., ::-1], pi[1][..., 1:][..., ::-1]
    afr = ctr[0][:, :, None, :] * pfr[..., None] - cti[0][:, :, None, :] * pfi[..., None]
    afi = ctr[0][:, :, None, :] * pfi[..., None] + cti[0][:, :, None, :] * pfr[..., None]
    abr = ctr[1][:, :, None, :] * pbr[..., None] - cti[1][:, :, None, :] * pbi[..., None]
    abi = ctr[1][:, :, None, :] * pbi[..., None] + cti[1][:, :, None, :] * pbr[..., None]
    kinds_w = [wfr, wfi, wbr, wbi]
    kinds_v = [a.reshape(SG, SP, CH * SH) for a in (afr, -afi, abr, -abi)]
    ws = jnp.zeros((NPAIR, 2, CH * SH, 4 * LANES), F32)
    vs = jnp.zeros((NPAIR, 2, 4 * LANES, CH * SH), F32)
    for k in range(4):
        wk = kinds_w[k].reshape(NPAIR, 2, CH * SH, SP)
        vk = kinds_v[k].reshape(NPAIR, 2, SP, CH * SH)
        for q in range(2):
            ws = ws.at[:, q, :, k * LANES + q * SP:k * LANES + (q + 1) * SP].set(wk[:, q])
            vs = vs.at[:, q, k * LANES + q * SP:k * LANES + (q + 1) * SP, :].set(vk[:, q])
    a16 = jnp.stack([pr[0][..., CH], pi[0][..., CH], pr[1][..., CH], pi[1][..., CH]], 0)
    a16 = a16.reshape(4, NPAIR, 1, LANES)
    return m.astype(BF16), ws.astype(BF16), vs.astype(BF16), a16


def _ssm_kernel(u_ref, m_ref, ws_ref, vs_ref, a_ref, h0_ref, y_ref, fin_ref, s_ref, ef_ref):
    u0, u1 = u_ref[0], u_ref[1]
    s = (jnp.dot(u0, ws_ref[0], preferred_element_type=F32)
         + jnp.dot(u1, ws_ref[1], preferred_element_type=F32))
    for k in range(4):
        s_ref[k] = s[:, k * LANES:(k + 1) * LANES]

    def scan(kr, ki, first, nrows, nsteps, reverse, er, ei):
        ar, ai = a_ref[kr], a_ref[ki]
        order = range(nsteps - 1, -1, -1) if reverse else range(nsteps)
        for c in order:
            rows = slice(first + c * nrows, first + (c + 1) * nrows)
            ef_ref[kr, rows, :] = er
            ef_ref[ki, rows, :] = ei
            sr, si = s_ref[kr, rows, :], s_ref[ki, rows, :]
            er, ei = ar * er - ai * ei + sr, ar * ei + ai * er + si
        return er, ei

    zc = jnp.zeros((BATCH, LANES), F32)
    for kr, ki, rev in ((0, 1, False), (2, 3, True)):
        er, ei = scan(kr, ki, 0, BATCH, CPS_CTX, rev, zc, zc)
        fin_ref[kr] = er
        fin_ref[ki] = ei
        scan(kr, ki, NCH_CTX, DEC_BATCH, CPS_LAT, rev, h0_ref[kr], h0_ref[ki])

    ef = jnp.concatenate([ef_ref[k] for k in range(4)], axis=1).astype(BF16)
    for q, uq in ((0, u0), (1, u1)):
        y_ref[q] = (jnp.dot(uq, m_ref[q], preferred_element_type=F32)
                    + jnp.dot(ef, vs_ref[q], preferred_element_type=F32))


def _ssm(u_chunks, m, ws, vs, a16, h0):
    pair = lambda j: (j, 0, 0)
    return pl.pallas_call(
        _ssm_kernel,
        grid=(NPAIR,),
        in_specs=[pl.BlockSpec((2, NCH, CH * SH), pair),
                  pl.BlockSpec((2, CH * SH, CH * SH), pair),
                  pl.BlockSpec((None, 2, CH * SH, 4 * LANES), lambda j: (j, 0, 0, 0)),
                  pl.BlockSpec((None, 2, 4 * LANES, CH * SH), lambda j: (j, 0, 0, 0)),
                  pl.BlockSpec((4, None, 1, LANES), lambda j: (0, j, 0, 0)),
                  pl.BlockSpec((4, None, DEC_BATCH, LANES), lambda j: (0, j, 0, 0))],
        out_specs=[pl.BlockSpec((2, NCH, CH * SH), pair),
                   pl.BlockSpec((4, None, BATCH, LANES), lambda j: (0, j, 0, 0))],
        out_shape=[jax.ShapeDtypeStruct((SG, NCH, CH * SH), F32),
                   jax.ShapeDtypeStruct((4, NPAIR, BATCH, LANES), F32)],
        scratch_shapes=[pltpu.VMEM((4, NCH, LANES), F32), pltpu.VMEM((4, NCH, LANES), F32)],
        compiler_params=_cparams(), name="s5_chunked",
    )(u_chunks, m, ws, vs, a16, h0)


def _to_chunks(u):
    a = u[:N_CTX].reshape(BATCH, CPS_CTX, CH, SG, SH).transpose(3, 1, 0, 2, 4).reshape(SG, NCH_CTX, CH * SH)
    b = u[N_CTX:].reshape(DEC_BATCH, CPS_LAT, CH, SG, SH).transpose(3, 1, 0, 2, 4).reshape(SG, NCH - NCH_CTX, CH * SH)
    return jnp.concatenate([a, b], axis=1)


def _from_chunks(y):
    a = y[:, :NCH_CTX].reshape(SG, CPS_CTX, BATCH, CH, SH).transpose(2, 1, 3, 0, 4).reshape(N_CTX, SW)
    b = y[:, NCH_CTX:].reshape(SG, CPS_LAT, DEC_BATCH, CH, SH).transpose(2, 1, 3, 0, 4).reshape(N_LAT, SW)
    return jnp.concatenate([a, b], axis=0)


def _merge_kernel(x_ref, bfc_ref, bfl_ref, bc_ref, ys_ref, gate_ref, mod_ref, wg_ref, bg_ref, wso_ref, wo_ref,
                  n2_ref, rwh_ref, rwl_ref, rb_ref, xo_ref, h_ref, ti_ref, tp_ref):
    i = pl.program_id(0)
    g = jax.nn.gelu(ys_ref[...])
    gl = jnp.dot(g.astype(BF16), wg_ref[...], preferred_element_type=F32) + bg_ref[...]
    y2 = g * _sigmoid(gl)
    brs = jnp.dot(y2.astype(BF16), wso_ref[...], preferred_element_type=F32)
    brf = jnp.where(i < CTX_BLOCKS, bfc_ref[...], bfl_ref[...]).astype(F32)
    mixed = (gate_ref[:, 0:D].astype(F32) * brf
             + gate_ref[:, D:2 * D].astype(F32) * bc_ref[...].astype(F32)
             + gate_ref[:, 2 * D:3 * D].astype(F32) * brs)
    g1 = mod_ref[:, 2 * D:3 * D]
    x = x_ref[...] + g1 * jnp.dot(mixed.astype(BF16), wo_ref[...], preferred_element_type=F32)
    xo_ref[...] = x
    sh2 = mod_ref[:, 3 * D:4 * D]
    sc2 = mod_ref[:, 4 * D:5 * D]
    ms = jnp.mean(x * x, axis=-1, keepdims=True)
    h = (x * lax.rsqrt(ms + EPS)) * n2_ref[...] * (1.0 + sc2) + sh2
    _std_to_rows(h_ref, h, TB)
    hh = h.astype(BF16)
    hl = (h - hh.astype(F32)).astype(BF16)
    logits = (jnp.dot(hh, rwh_ref[...], preferred_element_type=F32)
              + jnp.dot(hl, rwh_ref[...], preferred_element_type=F32)
              + jnp.dot(hh, rwl_ref[...], preferred_element_type=F32)) + rb_ref[...]
    lane = lax.broadcasted_iota(jnp.int32, (TB, LANES), 1).astype(F32)
    vals, idxs = [], []
    l = logits
    for _ in range(TOPK):
        m = jnp.max(l, axis=-1, keepdims=True)
        idx = jnp.min(jnp.where(l == m, lane, float(LANES)), axis=-1, keepdims=True)
        vals.append(m)
        idxs.append(idx)
        l = jnp.where(lane == idx, -jnp.inf, l)
    es = [jnp.exp(v - vals[0]) for v in vals]
    tot = es[0] + es[1] + es[2] + es[3]
    ti = jnp.zeros((TB, LANES), F32)
    tp = jnp.zeros((TB, LANES), F32)
    for k in range(TOPK):
        ti = jnp.where(lane == float(k), idxs[k], ti)
        tp = jnp.where(lane == float(k), es[k] / tot, tp)
    ti_ref[...] = ti.astype(jnp.int32)
    tp_ref[...] = tp


def _merge(layer, x, brf_ctx, brf_lat, brc, ys, gates, mod, w_glu, b_glu, w_ssm_out, w_out, norm2_g,
           rw_hi, rw_lo, rb):
    row = lambda i: (i, 0)
    return pl.pallas_call(
        _merge_kernel,
        grid=(NB,),
        in_specs=[pl.BlockSpec((TB, D), row),
                  pl.BlockSpec((TB, D), lambda i: (jnp.minimum(i, CTX_BLOCKS - 1), 0)),
                  pl.BlockSpec((TB, D), lambda i: (jnp.maximum(i - CTX_BLOCKS, 0), 0)),
                  pl.BlockSpec((TB, D), row),
                  pl.BlockSpec((TB, SW), row),
                  pl.BlockSpec((TB, 3 * D), row),
                  pl.BlockSpec((None, 1, NMOD * D), lambda i: (_mod_row(i), 0, 0)),
                  _const_spec((SW, SW), layer), _const_spec((1, SW), layer), _const_spec((SW, D), layer),
                  _const_spec((D, D), layer), _const_spec((1, D), layer),
                  _const_spec((D, LANES), layer), _const_spec((D, LANES), layer), _const_spec((1, LANES), layer)],
        out_specs=[pl.BlockSpec((TB, D), row), pl.BlockSpec((TB * DC, LANES), row),
                   pl.BlockSpec((TB, LANES), row), pl.BlockSpec((TB, LANES), row)],
        out_shape=[jax.ShapeDtypeStruct((N, D), F32), jax.ShapeDtypeStruct((N * DC, LANES), F32),
                   jax.ShapeDtypeStruct((N, LANES), jnp.int32), jax.ShapeDtypeStruct((N, LANES), F32)],
        compiler_params=_cparams(), name="merge_router",
    )(x, brf_ctx, brf_lat, brc, ys, gates, mod, w_glu, b_glu, w_ssm_out, w_out, norm2_g, rw_hi, rw_lo, rb)


def _routing_tables(top_i, top_p):
    e_flat = top_i[:, :TOPK].reshape(-1)
    p_flat = top_p[:, :TOPK].reshape(-1)
    na = N * TOPK
    key = e_flat * 32768 + jnp.arange(na, dtype=jnp.int32)
    skey = jnp.sort(key)
    sidx = skey & 32767
    counts = jnp.sum((e_flat[:, None] == jnp.arange(NE, dtype=jnp.int32)[None, :]).astype(jnp.int32), axis=0)
    padded = ((counts + TM - 1) // TM) * TM
    gend = jnp.cumsum(padded)
    gstart = gend - padded
    cstart = jnp.cumsum(counts) - counts
    n_tiles = gend[-1] // TM
    t0 = jnp.minimum(jnp.arange(NT_MAX, dtype=jnp.int32), n_tiles - 1) * TM
    tile_e = jnp.sum((t0[:, None] >= gend[None, :]).astype(jnp.int32), axis=1)
    tile_e = jnp.minimum(tile_e, NE - 1)
    r = jnp.arange(R_MAX, dtype=jnp.int32)
    e_r = tile_e[r // TM]
    off = r - gstart[e_r]
    valid = jnp.logical_and(off < counts[e_r], r < gend[-1])
    j = jnp.clip(cstart[e_r] + off, 0, na - 1)
    sj = sidx[j]
    row_tok = jnp.where(valid, sj // TOPK, N).astype(jnp.int32)
    row_w = jnp.where(valid, p_flat[sj], 0.0).astype(F32)
    return row_tok, row_w, tile_e.astype(jnp.int32), n_tiles.reshape(1).astype(jnp.int32)


def _gather_kernel(tok_ref, nt_ref, h_ref, o_ref, g_ref):
    i = pl.program_id(0)

    @pl.when(i < nt_ref[0])
    def _():
        base = i * TM

        def body(r, carry):
            tok = jnp.minimum(tok_ref[base + r], N - 1)
            g_ref[pl.ds(pl.multiple_of(r * DC, DC), DC), :] = h_ref[pl.ds(pl.multiple_of(tok * DC, DC), DC), :]
            return carry

        lax.fori_loop(0, TM, body, 0, unroll=8)
        for c in range(DC):
            o_ref[:, c * LANES:(c + 1) * LANES] = g_ref[pl.ds(c, TM, stride=DC), :].astype(BF16)


def _gather(h_rows, row_tok, n_tiles):
    tile = lambda i, tok, nt: (jnp.minimum(i, nt[0] - 1), 0)
    return pl.pallas_call(
        _gather_kernel,
        grid_spec=pltpu.PrefetchScalarGridSpec(
            num_scalar_prefetch=2, grid=(NT_MAX,),
            in_specs=[pl.BlockSpec((N * DC, LANES), lambda i, tok, nt: (0, 0))],
            out_specs=pl.BlockSpec((TM, D), tile),
            scratch_shapes=[pltpu.VMEM((TM * DC, LANES), F32)]),
        out_shape=jax.ShapeDtypeStruct((R_MAX, D), BF16),
        compiler_params=_cparams(), name="moe_gather",
    )(row_tok, n_tiles, h_rows)


def _experts_kernel(tok_ref, w_ref, te_ref, nt_ref, x_ref, wgu_ref, bgu_ref, wdn_ref, bdn_ref, acc_ref, y_ref):
    i = pl.program_id(0)

    @pl.when(i == 0)
    def _():
        acc_ref[...] = jnp.zeros_like(acc_ref)

    @pl.when(i < nt_ref[0])
    def _():
        x = x_ref[...]
        y = jnp.broadcast_to(bdn_ref[...], (TM, D))
        for f in range(0, DFF, FC):
            gate = jnp.dot(x, wgu_ref[:, f:f + FC].astype(BF16), preferred_element_type=F32) + bgu_ref[:, f:f + FC]
            up = (jnp.dot(x, wgu_ref[:, DFF + f:DFF + f + FC].astype(BF16), preferred_element_type=F32)
                  + bgu_ref[:, DFF + f:DFF + f + FC])
            gate = jnp.minimum(gate, LIMIT)
            up = jnp.clip(up, -LIMIT, LIMIT)
            act = (up + 1.0) * (gate * _sigmoid(ALPHA * gate))
            y = y + jnp.dot(act.astype(BF16), wdn_ref[f:f + FC, :].astype(BF16), preferred_element_type=F32)
        _std_to_rows(y_ref, y, TM)
        base = i * TM

        def body(b, carry):
            r0 = b * SCAT
            toks = [tok_ref[base + r0 + j] for j in range(SCAT)]
            ws = [w_ref[base + r0 + j] for j in range(SCAT)]
            olds = [acc_ref[pl.ds(pl.multiple_of(t * DC, DC), DC), :] for t in toks]
            news = [olds[j] + ws[j] * y_ref[pl.ds(pl.multiple_of((r0 + j) * DC, DC), DC), :] for j in range(SCAT)]
            for j in range(SCAT):
                acc_ref[pl.ds(pl.multiple_of(toks[j] * DC, DC), DC), :] = news[j]
            return carry

        lax.fori_loop(0, TM // SCAT, body, 0)


def _experts(layer, x_sorted, row_tok, row_w, tile_e, n_tiles, w_gate_up, b_gate_up, w_down, b_down):
    tile = lambda i, tok, w, te, nt: (jnp.minimum(i, nt[0] - 1), 0)
    ex4 = lambda i, tok, w, te, nt: (layer, te[i], 0, 0)
    return pl.pallas_call(
        _experts_kernel,
        grid_spec=pltpu.PrefetchScalarGridSpec(
            num_scalar_prefetch=4, grid=(NT_MAX,),
            in_specs=[pl.BlockSpec((TM, D), tile),
                      pl.BlockSpec((None, None, D, 2 * DFF), ex4),
                      pl.BlockSpec((None, None, 1, 2 * DFF), ex4),
                      pl.BlockSpec((None, None, DFF, D), ex4),
                      pl.BlockSpec((None, None, 1, D), ex4)],
            out_specs=pl.BlockSpec(((N + SUB) * DC, LANES), lambda i, tok, w, te, nt: (0, 0)),
            scratch_shapes=[pltpu.VMEM((TM * DC, LANES), F32)]),
        out_shape=jax.ShapeDtypeStruct(((N + SUB) * DC, LANES), F32),
        compiler_params=_cparams(VMEM_MOE), name="moe_experts",
    )(row_tok, row_w, tile_e, n_tiles, x_sorted, w_gate_up, b_gate_up.reshape(DEPTH, NE, 1, 2 * DFF),
      w_down, b_down.reshape(DEPTH, NE, 1, D))


def _final_kernel(x_ref, moe_ref, mod_ref, g_ref, o_ref):
    x = x_ref[...] + mod_ref[:, 5 * D:6 * D] * _rows_to_std(moe_ref, TB)
    ms = jnp.mean(x * x, axis=-1, keepdims=True)
    o_ref[...] = (x * lax.rsqrt(ms + EPS)) * g_ref[...]


def _final(x, moe, mod, g):
    row = lambda i: (i, 0)
    return pl.pallas_call(
        _final_kernel,
        grid=(NB,),
        in_specs=[pl.BlockSpec((TB, D), row), pl.BlockSpec((TB * DC, LANES), row),
                  pl.BlockSpec((None, 1, NMOD * D), lambda i: (_mod_row(i), 0, 0)), _const_spec((1, D))],
        out_specs=pl.BlockSpec((TB, D), row),
        out_shape=jax.ShapeDtypeStruct((N, D), F32),
        compiler_params=_cparams(), name="final_norm",
    )(x, moe, mod, g)


def _grid_pos_embed(rows):
    quarter = D // 4
    freqs = jnp.exp(-math.log(10000.0) * jnp.arange(quarter, dtype=F32) / quarter)
    r = jnp.repeat(jnp.arange(rows, dtype=F32), GRID_W)
    col = jnp.tile(jnp.arange(GRID_W, dtype=F32), rows)
    ar = r[:, None] * freqs
    ac = col[:, None] * freqs
    return jnp.concatenate([jnp.sin(ar), jnp.cos(ar), jnp.sin(ac), jnp.cos(ac)], axis=-1)


def kernel(x_prompt, x_sample, state_ssm_re, state_ssm_im, c, c_ctx, w_mod, b_mod, norm1_g, norm2_g, w_in, b_in, w_four, conv_dw, conv_dw_b, conv_ln_g, conv_ln_b, w_conv_out, ssm_lam_re, ssm_lam_im, ssm_log_dt, ssm_b_re, ssm_b_im, ssm_c_re, ssm_c_im, ssm_d, w_ssm_glu, b_ssm_glu, w_ssm_out, w_out, router_w, router_b, w_gate_up, b_gate_up, w_down, b_down, final_norm_g):
    xs = x_sample + _grid_pos_embed(DEC_SEQ // GRID_W)[None]
    x = jnp.concatenate([x_prompt.reshape(N_CTX, D), xs.reshape(N_LAT, D)], axis=0)

    cond8 = jnp.concatenate([c_ctx[None, :], c, jnp.zeros((SUB - 1 - DEC_BATCH, D), F32)], axis=0)
    mod_all = _adaln(cond8, w_mod, b_mod)[:, :1 + DEC_BATCH].reshape(DEPTH, 1 + DEC_BATCH, 1, NMOD * D)

    w_in_b, w_four_b, w_conv_b = w_in.astype(BF16), w_four.astype(BF16), w_conv_out.astype(BF16)
    w_glu_b, w_sso_b, w_out_b = w_ssm_glu.astype(BF16), w_ssm_out.astype(BF16), w_out.astype(BF16)
    rw = jnp.pad(router_w, ((0, 0), (0, 0), (0, LANES - NE)))
    rw_hi = rw.astype(BF16)
    rw_lo = (rw - rw_hi.astype(F32)).astype(BF16)
    rb = jnp.pad(router_b, ((0, 0), (0, LANES - NE)), constant_values=-1e30).reshape(DEPTH, 1, LANES)
    r3 = lambda a: a.reshape(DEPTH, 1, a.shape[-1])

    fin_re, fin_im = [], []
    moe = None
    mod_prev = None
    for l in range(DEPTH):
        mod = mod_all[l]
        x, zf, v, u, gates = _inproj(l, x, moe, mod_prev, mod, r3(norm1_g), w_in_b, r3(b_in))
        brf_ctx = _fourier(l, zf, w_four_b, SEQ, 0, BATCH)
        brf_lat = _fourier(l, zf, w_four_b, DEC_SEQ, N_CTX // DEC_SEQ, DEC_BATCH)
        brc = _conv(l, v, conv_dw, r3(conv_dw_b), r3(conv_ln_g), r3(conv_ln_b), w_conv_b)

        m, ws, vs, a16 = _ssm_params(ssm_lam_re[l], ssm_lam_im[l], ssm_log_dt[l], ssm_b_re[l], ssm_b_im[l],
                                     ssm_c_re[l], ssm_c_im[l], ssm_d[l])
        sr, si = state_ssm_re[:, l], state_ssm_im[:, l]
        h0 = jnp.stack([sr[:, 0], si[:, 0], sr[:, 1], si[:, 1]], axis=0)
        h0 = h0.reshape(4, DEC_BATCH, NPAIR, LANES).transpose(0, 2, 1, 3)
        y_chunks, fin = _ssm(_to_chunks(u), m, ws, vs, a16, h0)
        ys = _from_chunks(y_chunks)
        fin = fin.transpose(0, 2, 1, 3).reshape(4, BATCH, SG, SP)
        fin_re.append(jnp.stack([fin[0], fin[2]], axis=1))
        fin_im.append(jnp.stack([fin[1], fin[3]], axis=1))

        x, h_rows, top_i, top_p = _merge(l, x, brf_ctx, brf_lat, brc, ys, gates, mod, w_glu_b, r3(b_ssm_glu),
                                         w_sso_b, w_out_b, r3(norm2_g), rw_hi, rw_lo, rb)
        row_tok, row_w, tile_e, n_tiles = _routing_tables(top_i, top_p)
        x_sorted = _gather(h_rows, row_tok, n_tiles)
        moe = _experts(l, x_sorted, row_tok, row_w, tile_e, n_tiles, w_gate_up, b_gate_up, w_down, b_down)
        mod_prev = mod

    y = _final(x, moe, mod_prev, final_norm_g.reshape(1, D))
    new_re = jnp.stack(fin_re, axis=1)
    new_im = jnp.stack(fin_im, axis=1)
    return (y[:N_CTX].reshape(BATCH, SEQ, D), y[N_CTX:].reshape(DEC_BATCH, DEC_SEQ, D), new_re, new_im)
```

```python
import functools
import math

import numpy as np
import jax
import jax.numpy as jnp
from jax import lax
from jax.experimental import pallas as pl
from jax.experimental.pallas import tpu as pltpu

F32 = jnp.float32
BF16 = jnp.bfloat16
HI = lax.Precision.HIGHEST

D = 1024
BATCH, SEQ = 16, 256
DEC_BATCH, DEC_SEQ = 2, 1024
DEPTH = 4
N_CTX = BATCH * SEQ
N_LAT = DEC_BATCH * DEC_SEQ
N = N_CTX + N_LAT
GRID_W = 64
FW, FG = 512, 4
FGD = FW // FG
CC, TAPS = 512, 31
PAD = TAPS // 2
SW, SH = 512, 16
SG = SW // SH
SP = 64
IN_COLS = FW + 2 * CC + SW + 3 * D
NE, TOPK, DFF = 32, 4, 1024
LIMIT, ALPHA = 7.0, 1.702
EPS = 1e-6
NMOD = 6

TB = 256
NB = N // TB
CTX_BLOCKS = N_CTX // TB
LAT_BLOCKS_PER_SEQ = DEC_SEQ // TB
CH = 16
NCH = N // CH
NCH_CTX = N_CTX // CH
CPS_CTX = SEQ // CH
CPS_LAT = DEC_SEQ // CH
NPAIR = SG // 2
TM = 256
NT_MAX = (N * TOPK + NE * (TM - 1) + TM - 1) // TM
R_MAX = NT_MAX * TM
SUB = 8
LANES = 128
DC = D // LANES
SCAT = 16
FC = 512
VMEM_DEFAULT = 48 * 1024 * 1024
VMEM_MOE = 62 * 1024 * 1024


def _cparams(vmem=VMEM_DEFAULT, ndim=1):
    return pltpu.CompilerParams(dimension_semantics=("arbitrary",) * ndim, vmem_limit_bytes=vmem)


def _const_spec(shape, layer=None):
    if layer is None:
        zeros = (0,) * len(shape)
        return pl.BlockSpec(shape, lambda *_: zeros)
    zeros = (0,) * len(shape)
    return pl.BlockSpec((None,) + tuple(shape), lambda *_: (layer,) + zeros, pipeline_mode=pl.Buffered(1))


def _mod_row(i):
    return jnp.where(i < CTX_BLOCKS, 0, 1 + (i - CTX_BLOCKS) // LAT_BLOCKS_PER_SEQ)


def _sigmoid(x):
    return 1.0 / (1.0 + jnp.exp(-x))


def _rows_to_std(ref, rows):
    return jnp.concatenate([ref[pl.ds(c, rows, stride=DC), :] for c in range(DC)], axis=1)


def _std_to_rows(ref, val, rows):
    for c in range(DC):
        ref[pl.ds(c, rows, stride=DC), :] = val[:, c * LANES:(c + 1) * LANES]


NM = SW // LANES
GPV = LANES // SH
CPB = TB // CH


def _lane_group():
    return lax.broadcasted_iota(jnp.int32, (CPB, LANES), 1) // SH


def _tokens_to_chunks(u, slab_ref, out_ref):
    for m in range(NM):
        slab_ref[m] = u[:, m * LANES:(m + 1) * LANES]
    grp = _lane_group()
    for m in range(NM):
        for kk in range(CH // GPV):
            zs = [slab_ref[m, pl.ds(kk * GPV + k8, CPB, stride=CH), :] for k8 in range(GPV)]
            for r in range(GPV):
                acc = None
                for k8 in range(GPV):
                    sh = ((k8 - r) % GPV) * SH
                    piece = zs[k8] if sh == 0 else pltpu.roll(zs[k8], sh, 1)
                    acc = piece if acc is None else jnp.where(grp == k8, piece, acc)
                out_ref[m * GPV + r, :, kk * LANES:(kk + 1) * LANES] = acc.astype(out_ref.dtype)


def _chunks_to_tokens(y_ref, slab_ref):
    grp = _lane_group()
    for m in range(NM):
        for kk in range(CH // GPV):
            ys = [y_ref[m * GPV + r, :, kk * LANES:(kk + 1) * LANES] for r in range(GPV)]
            for t8 in range(GPV):
                acc = None
                for r in range(GPV):
                    sh = ((r - t8) % GPV) * SH
                    piece = ys[r] if sh == 0 else pltpu.roll(ys[r], sh, 1)
                    acc = piece if acc is None else jnp.where(grp == r, piece, acc)
                slab_ref[m, pl.ds(kk * GPV + t8, CPB, stride=CH), :] = acc
    return jnp.concatenate([slab_ref[m] for m in range(NM)], axis=1)


MOD_COLS = 1536


def _mod_kernel(cond_ref, w_ref, b_ref, o_ref):
    c = cond_ref[...]
    s = c * _sigmoid(c)
    o_ref[...] = jnp.dot(s.astype(BF16), w_ref[...].astype(BF16), preferred_element_type=F32) + b_ref[...]


def _adaln(cond8, w_mod, b_mod):
    nc = NMOD * D // MOD_COLS
    return pl.pallas_call(
        _mod_kernel,
        grid=(DEPTH, nc),
        in_specs=[
            pl.BlockSpec((SUB, D), lambda l, j: (0, 0)),
            pl.BlockSpec((None, D, MOD_COLS), lambda l, j: (l, 0, j)),
            pl.BlockSpec((None, 1, MOD_COLS), lambda l, j: (l, 0, j)),
        ],
        out_specs=pl.BlockSpec((None, SUB, MOD_COLS), lambda l, j: (l, 0, j)),
        out_shape=jax.ShapeDtypeStruct((DEPTH, SUB, NMOD * D), F32),
        compiler_params=_cparams(ndim=2),
        name="adaln_mod",
    )(cond8, w_mod, b_mod.reshape(DEPTH, 1, NMOD * D))


def _inproj_kernel(has_moe, *refs):
    if has_moe:
        (x_ref, moe_ref, modp_ref, mod_ref, g_ref, w_ref, b_ref,
         xo_ref, zf_ref, v_ref, u_ref, gate_ref, slab_ref) = refs
        x = x_ref[...] + modp_ref[:, 5 * D:6 * D] * _rows_to_std(moe_ref, TB)
        xo_ref[...] = x
    else:
        x_ref, mod_ref, g_ref, w_ref, b_ref, zf_ref, v_ref, u_ref, gate_ref, slab_ref = refs
        x = x_ref[...]
    sh1 = mod_ref[:, 0:D]
    sc1 = mod_ref[:, D:2 * D]
    ms = jnp.mean(x * x, axis=-1, keepdims=True)
    h = (x * lax.rsqrt(ms + EPS)) * g_ref[...] * (1.0 + sc1) + sh1
    z = jnp.dot(h.astype(BF16), w_ref[...], preferred_element_type=F32) + b_ref[...]
    o = 0
    zf_ref[...] = z[:, o:o + FW].astype(BF16)
    o += FW
    za = z[:, o:o + CC]
    zb = z[:, o + CC:o + 2 * CC]
    v_ref[...] = (za * _sigmoid(zb)).astype(BF16)
    o += 2 * CC
    _tokens_to_chunks(z[:, o:o + SW], slab_ref, u_ref)
    o += SW
    gate_ref[...] = _sigmoid(z[:, o:]).astype(BF16)


def _inproj(layer, x, moe, mod_prev, mod, norm_g, w_in, b_in):
    has_moe = moe is not None
    row = lambda i: (i, 0)
    mod_spec = pl.BlockSpec((None, 1, NMOD * D), lambda i: (_mod_row(i), 0, 0))
    in_specs = [pl.BlockSpec((TB, D), row)]
    args = [x]
    if has_moe:
        in_specs += [pl.BlockSpec((TB * DC, LANES), row), mod_spec]
        args += [moe, mod_prev]
    in_specs += [mod_spec, _const_spec((1, D), layer), _const_spec((D, IN_COLS), layer),
                 _const_spec((1, IN_COLS), layer)]
    args += [mod, norm_g, w_in, b_in]
    out_specs = [pl.BlockSpec((TB, FW), row), pl.BlockSpec((TB, CC), row),
                 pl.BlockSpec((SG, CPB, CH * SH), lambda i: (0, i, 0)), pl.BlockSpec((TB, 3 * D), row)]
    out_shape = [jax.ShapeDtypeStruct((N, FW), BF16), jax.ShapeDtypeStruct((N, CC), BF16),
                 jax.ShapeDtypeStruct((SG, NCH, CH * SH), BF16), jax.ShapeDtypeStruct((N, 3 * D), BF16)]
    if has_moe:
        out_specs = [pl.BlockSpec((TB, D), row)] + out_specs
        out_shape = [jax.ShapeDtypeStruct((N, D), F32)] + out_shape
    outs = pl.pallas_call(
        functools.partial(_inproj_kernel, has_moe),
        grid=(NB,), in_specs=in_specs, out_specs=out_specs, out_shape=out_shape,
        scratch_shapes=[pltpu.VMEM((NM, TB, LANES), F32)],
        compiler_params=_cparams(), name=f"inproj_{int(has_moe)}",
    )(*args)
    if has_moe:
        return outs
    return [x] + list(outs)


def _fourier_kernel(scale, zf_ref, csbd_ref, csl_ref, wf_ref, o_ref):
    t = jnp.dot(zf_ref[...], csbd_ref[...], preferred_element_type=F32).astype(BF16)
    ts = jnp.concatenate([t[:, :FW], t[:, FW:]], axis=0)
    y = jnp.dot(csl_ref[...], ts, preferred_element_type=F32) * scale
    o_ref[...] = jnp.dot(y.astype(BF16), wf_ref[...], preferred_element_type=F32).astype(BF16)


def _dft_consts(L):
    k = np.arange(L)
    ang = 2.0 * np.pi * ((k[:, None] * k[None, :]) % L) / L
    csl = np.concatenate([np.cos(ang), -np.sin(ang)], axis=1)
    m = np.arange(FGD)
    angc = 2.0 * np.pi * ((m[:, None] * m[None, :]) % FGD) / FGD
    cbd = np.kron(np.eye(FG), np.cos(angc))
    sbd = np.kron(np.eye(FG), np.sin(angc))
    csbd = np.concatenate([cbd, sbd], axis=1)
    return jnp.asarray(csl, F32).astype(BF16), jnp.asarray(csbd, F32).astype(BF16)


def _fourier(layer, zf, w_four, L, first_block, nseq):
    csl, csbd = _dft_consts(L)
    scale = 1.0 / math.sqrt(L * FGD)
    return pl.pallas_call(
        functools.partial(_fourier_kernel, scale),
        grid=(nseq,),
        in_specs=[pl.BlockSpec((L, FW), lambda i: (i + first_block, 0)),
                  _const_spec((FW, 2 * FW)), _const_spec((L, 2 * L)), _const_spec((FW, D), layer)],
        out_specs=pl.BlockSpec((L, D), lambda i: (i, 0)),
        out_shape=jax.ShapeDtypeStruct((nseq * L, D), BF16),
        compiler_params=_cparams(), name=f"fourier_{L}",
    )(zf, csbd, csl, w_four)


HALO = 16
CONV_ROWS = 64


def _conv_kernel(vp_ref, vc_ref, vn_ref, cw_ref, cb_ref, lg_ref, lb_ref, wc_ref, o_ref, pad_ref, act_ref):
    i = pl.program_id(0)
    pos = (i - CTX_BLOCKS) % LAT_BLOCKS_PER_SEQ
    lat = i >= CTX_BLOCKS
    has_prev = jnp.logical_and(lat, pos != 0)
    has_next = jnp.logical_and(lat, pos != LAT_BLOCKS_PER_SEQ - 1)
    prev = vp_ref[TB - HALO:, :].astype(F32)
    nxt = vn_ref[:HALO, :].astype(F32)
    pad_ref[0:HALO, :] = jnp.where(has_prev, prev, 0.0)
    pad_ref[HALO:HALO + TB, :] = vc_ref[...].astype(F32)
    pad_ref[HALO + TB:, :] = jnp.where(has_next, nxt, 0.0)
    for r0 in range(0, TB, CONV_ROWS):
        acc = jnp.broadcast_to(cb_ref[...], (CONV_ROWS, CC))
        for j in range(TAPS):
            s = r0 + HALO - PAD + j
            acc = acc + pad_ref[s:s + CONV_ROWS, :] * cw_ref[j:j + 1, :]
        mu = jnp.mean(acc, axis=-1, keepdims=True)
        xc = acc - mu
        var = jnp.mean(xc * xc, axis=-1, keepdims=True)
        y = xc * lax.rsqrt(var + EPS) * lg_ref[...] + lb_ref[...]
        act_ref[r0:r0 + CONV_ROWS, :] = (y * _sigmoid(y)).astype(BF16)
    o_ref[...] = jnp.dot(act_ref[...], wc_ref[...], preferred_element_type=F32).astype(BF16)


def _conv(layer, v, conv_dw, conv_b, ln_g, ln_b, w_conv_out):
    return pl.pallas_call(
        _conv_kernel,
        grid=(NB,),
        in_specs=[pl.BlockSpec((TB, CC), lambda i: (jnp.maximum(i - 1, 0), 0)),
                  pl.BlockSpec((TB, CC), lambda i: (i, 0)),
                  pl.BlockSpec((TB, CC), lambda i: (jnp.minimum(i + 1, NB - 1), 0)),
                  _const_spec((TAPS, CC), layer), _const_spec((1, CC), layer), _const_spec((1, CC), layer),
                  _const_spec((1, CC), layer), _const_spec((CC, D), layer)],
        out_specs=pl.BlockSpec((TB, D), lambda i: (i, 0)),
        out_shape=jax.ShapeDtypeStruct((N, D), BF16),
        scratch_shapes=[pltpu.VMEM((TB + 2 * HALO, CC), F32), pltpu.VMEM((TB, CC), BF16)],
        compiler_params=_cparams(), name="conv_branch",
    )(v, v, v, conv_dw, conv_b, ln_g, ln_b, w_conv_out)


def _ssm_params(lam_re, lam_im, log_dt, b_re, b_im, c_re, c_im, d):
    dt = jnp.exp(log_dt)[..., None]
    ldr, ldi = lam_re * dt, lam_im * dt
    j = jnp.arange(CH + 1, dtype=F32)
    mag = jnp.exp(ldr[..., None] * j)
    pr, pi = mag * jnp.cos(ldi[..., None] * j), mag * jnp.sin(ldi[..., None] * j)
    nr, ni = pr[..., 1] - 1.0, pi[..., 1]
    den = lam_re * lam_re + lam_im * lam_im
    qr, qi = (nr * lam_re + ni * lam_im) / den, (ni * lam_re - nr * lam_im) / den
    bbr = qr[..., None] * b_re - qi[..., None] * b_im
    bbi = qr[..., None] * b_im + qi[..., None] * b_re
    xr = pr[..., :CH, None] * bbr[..., None, :] - pi[..., :CH, None] * bbi[..., None, :]
    xi = pr[..., :CH, None] * bbi[..., None, :] + pi[..., :CH, None] * bbr[..., None, :]
    kk = (jnp.einsum("dghp,dgpji->dgjhi", c_re, xr, precision=HI)
          - jnp.einsum("dghp,dgpji->dgjhi", c_im, xi, precision=HI))
    lag = np.arange(CH)[:, None, None]
    ki = np.arange(CH)[None, :, None]
    ti = np.arange(CH)[None, None, :]
    self_f = (ti - ki == lag).astype(np.float32)
    self_b = (ki - ti == lag).astype(np.float32)
    mm = (jnp.einsum("gjhi,jkt->gkith", kk[0], self_f, precision=HI)
          + jnp.einsum("gjhi,jkt->gkith", kk[1], self_b, precision=HI))
    eye_t = np.eye(CH, dtype=np.float32)
    eye_h = np.eye(SH, dtype=np.float32)
    dd = d.reshape(SG, SH)
    mm = mm + dd[:, None, :, None, None] * eye_t[None, :, None, :, None] * eye_h[None, None, :, None, :]
    m = mm.reshape(SG, CH * SH, CH * SH)
    wfr = jnp.transpose(xr[0][:, :, ::-1, :], (0, 2, 3, 1)).reshape(SG, CH * SH, SP)
    wfi = jnp.transpose(xi[0][:, :, ::-1, :], (0, 2, 3, 1)).reshape(SG, CH * SH, SP)
    wbr = jnp.transpose(xr[1], (0, 2, 3, 1)).reshape(SG, CH * SH, SP)
    wbi = jnp.transpose(xi[1], (0, 2, 3, 1)).reshape(SG, CH * SH, SP)
    ctr, cti = jnp.transpose(c_re, (0, 1, 3, 2)), jnp.transpose(c_im, (0, 1, 3, 2))
    pfr, pfi = pr[0][..., 1:], pi[0][..., 1:]
    pbr, pbi = pr[1][..., 1:][..., ::-1], pi[1][..., 1:][..., ::-1]
    afr = ctr[0][:, :, None, :] * pfr[..., None] - cti[0][:, :, None, :] * pfi[..., None]
    afi = ctr[0][:, :, None, :] * pfi[..., None] + cti[0][:, :, None, :] * pfr[..., None]
    abr = ctr[1][:, :, None, :] * pbr[..., None] - cti[1][:, :, None, :] * pbi[..., None]
    abi = ctr[1][:, :, None, :] * pbi[..., None] + cti[1][:, :, None, :] * pbr[..., None]
    kinds_w = [wfr, wfi, wbr, wbi]
    kinds_v = [a.reshape(SG, SP, CH * SH) for a in (afr, -afi, abr, -abi)]
    eye_q = np.eye(2, dtype=np.float32)
    w4 = jnp.stack(kinds_w, axis=2).reshape(NPAIR, 2, CH * SH, 4, 1, SP)
    ws = (w4 * eye_q[None, :, None, None, :, None]).reshape(NPAIR, 2, CH * SH, 4 * LANES)
    v4 = jnp.stack(kinds_v, axis=1).reshape(NPAIR, 2, 4, 1, SP, CH * SH)
    vs = (v4 * eye_q[None, :, None, :, None, None]).reshape(NPAIR, 2, 4 * LANES, CH * SH)
    a16 = jnp.stack([pr[0][..., CH], pi[0][..., CH], pr[1][..., CH], pi[1][..., CH]], 0)
    a16 = a16.reshape(4, NPAIR, 1, LANES)
    return m.astype(BF16), ws.astype(BF16), vs.astype(BF16), a16


def _ssm_kernel(u_ref, m_ref, ws_ref, vs_ref, a_ref, h0_ref, y_ref, fin_ref, s_ref, ef_ref):
    u0, u1 = u_ref[0], u_ref[1]
    s = (jnp.dot(u0, ws_ref[0], preferred_element_type=F32)
         + jnp.dot(u1, ws_ref[1], preferred_element_type=F32))
    for k in range(4):
        s_ref[k] = s[:, k * LANES:(k + 1) * LANES]

    def scan(kr, ki, first, nrows, nsteps, reverse, er, ei):
        ar, ai = a_ref[kr], a_ref[ki]
        order = range(nsteps - 1, -1, -1) if reverse else range(nsteps)
        for c in order:
            rows = pl.ds(first + c, nrows, stride=nsteps)
            ef_ref[kr, rows, :] = er
            ef_ref[ki, rows, :] = ei
            sr, si = s_ref[kr, rows, :], s_ref[ki, rows, :]
            er, ei = ar * er - ai * ei + sr, ar * ei + ai * er + si
        return er, ei

    zc = jnp.zeros((BATCH, LANES), F32)
    for kr, ki, rev in ((0, 1, False), (2, 3, True)):
        er, ei = scan(kr, ki, 0, BATCH, CPS_CTX, rev, zc, zc)
        fin_ref[kr] = er
        fin_ref[ki] = ei
        scan(kr, ki, NCH_CTX, DEC_BATCH, CPS_LAT, rev, h0_ref[kr], h0_ref[ki])

    ef = jnp.concatenate([ef_ref[k] for k in range(4)], axis=1).astype(BF16)
    for q, uq in ((0, u0), (1, u1)):
        y_ref[q] = (jnp.dot(uq, m_ref[q], preferred_element_type=F32)
                    + jnp.dot(ef, vs_ref[q], preferred_element_type=F32))


def _ssm(layer, u_chunks, m, ws, vs, a16, h0):
    pair = lambda j: (j, 0, 0)
    return pl.pallas_call(
        _ssm_kernel,
        grid=(NPAIR,),
        in_specs=[pl.BlockSpec((2, NCH, CH * SH), pair),
                  pl.BlockSpec((None, 2, CH * SH, CH * SH), lambda j: (layer, j, 0, 0)),
                  pl.BlockSpec((None, None, 2, CH * SH, 4 * LANES), lambda j: (layer, j, 0, 0, 0)),
                  pl.BlockSpec((None, None, 2, 4 * LANES, CH * SH), lambda j: (layer, j, 0, 0, 0)),
                  pl.BlockSpec((None, 4, None, 1, LANES), lambda j: (layer, 0, j, 0, 0)),
                  pl.BlockSpec((None, 4, None, DEC_BATCH, LANES), lambda j: (layer, 0, j, 0, 0))],
        out_specs=[pl.BlockSpec((2, NCH, CH * SH), pair),
                   pl.BlockSpec((4, None, BATCH, LANES), lambda j: (0, j, 0, 0))],
        out_shape=[jax.ShapeDtypeStruct((SG, NCH, CH * SH), F32),
                   jax.ShapeDtypeStruct((4, NPAIR, BATCH, LANES), F32)],
        scratch_shapes=[pltpu.VMEM((4, NCH, LANES), F32), pltpu.VMEM((4, NCH, LANES), F32)],
        compiler_params=_cparams(), name="s5_chunked",
    )(u_chunks, m, ws, vs, a16, h0)


def _merge_kernel(x_ref, bfc_ref, bfl_ref, bc_ref, ys_ref, gate_ref, mod_ref, wg_ref, bg_ref, wso_ref, wo_ref,
                  n2_ref, rwh_ref, rwl_ref, rb_ref, xo_ref, h_ref, ti_ref, tp_ref, slab_ref):
    i = pl.program_id(0)
    g = jax.nn.gelu(_chunks_to_tokens(ys_ref, slab_ref))
    gl = jnp.dot(g.astype(BF16), wg_ref[...], preferred_element_type=F32) + bg_ref[...]
    y2 = g * _sigmoid(gl)
    brs = jnp.dot(y2.astype(BF16), wso_ref[...], preferred_element_type=F32)
    brf = jnp.where(i < CTX_BLOCKS, bfc_ref[...], bfl_ref[...]).astype(F32)
    mixed = (gate_ref[:, 0:D].astype(F32) * brf
             + gate_ref[:, D:2 * D].astype(F32) * bc_ref[...].astype(F32)
             + gate_ref[:, 2 * D:3 * D].astype(F32) * brs)
    g1 = mod_ref[:, 2 * D:3 * D]
    x = x_ref[...] + g1 * jnp.dot(mixed.astype(BF16), wo_ref[...], preferred_element_type=F32)
    xo_ref[...] = x
    sh2 = mod_ref[:, 3 * D:4 * D]
    sc2 = mod_ref[:, 4 * D:5 * D]
    ms = jnp.mean(x * x, axis=-1, keepdims=True)
    h = (x * lax.rsqrt(ms + EPS)) * n2_ref[...] * (1.0 + sc2) + sh2
    _std_to_rows(h_ref, h, TB)
    hh = h.astype(BF16)
    hl = (h - hh.astype(F32)).astype(BF16)
    logits = (jnp.dot(hh, rwh_ref[...], preferred_element_type=F32)
              + jnp.dot(hl, rwh_ref[...], preferred_element_type=F32)
              + jnp.dot(hh, rwl_ref[...], preferred_element_type=F32)) + rb_ref[...]
    lane = lax.broadcasted_iota(jnp.int32, (TB, LANES), 1).astype(F32)
    vals, idxs = [], []
    l = logits
    for _ in range(TOPK):
        m = jnp.max(l, axis=-1, keepdims=True)
        idx = jnp.min(jnp.where(l == m, lane, float(LANES)), axis=-1, keepdims=True)
        vals.append(m)
        idxs.append(idx)
        l = jnp.where(lane == idx, -jnp.inf, l)
    es = [jnp.exp(v - vals[0]) for v in vals]
    tot = es[0] + es[1] + es[2] + es[3]
    ti = jnp.zeros((TB, LANES), F32)
    tp = jnp.zeros((TB, LANES), F32)
    for k in range(TOPK):
        ti = jnp.where(lane == float(k), idxs[k], ti)
        tp = jnp.where(lane == float(k), es[k] / tot, tp)
    ti_ref[...] = ti.astype(jnp.int32)
    tp_ref[...] = tp


def _merge(layer, x, brf_ctx, brf_lat, brc, ys, gates, mod, w_glu, b_glu, w_ssm_out, w_out, norm2_g,
           rw_hi, rw_lo, rb):
    row = lambda i: (i, 0)
    return pl.pallas_call(
        _merge_kernel,
        grid=(NB,),
        in_specs=[pl.BlockSpec((TB, D), row),
                  pl.BlockSpec((TB, D), lambda i: (jnp.minimum(i, CTX_BLOCKS - 1), 0)),
                  pl.BlockSpec((TB, D), lambda i: (jnp.maximum(i - CTX_BLOCKS, 0), 0)),
                  pl.BlockSpec((TB, D), row),
                  pl.BlockSpec((SG, CPB, CH * SH), lambda i: (0, i, 0)),
                  pl.BlockSpec((TB, 3 * D), row),
                  pl.BlockSpec((None, 1, NMOD * D), lambda i: (_mod_row(i), 0, 0)),
                  _const_spec((SW, SW), layer), _const_spec((1, SW), layer), _const_spec((SW, D), layer),
                  _const_spec((D, D), layer), _const_spec((1, D), layer),
                  _const_spec((D, LANES), layer), _const_spec((D, LANES), layer), _const_spec((1, LANES), layer)],
        out_specs=[pl.BlockSpec((TB, D), row), pl.BlockSpec((TB * DC, LANES), row),
                   pl.BlockSpec((TB, LANES), row), pl.BlockSpec((TB, LANES), row)],
        out_shape=[jax.ShapeDtypeStruct((N, D), F32), jax.ShapeDtypeStruct((N * DC, LANES), F32),
                   jax.ShapeDtypeStruct((N, LANES), jnp.int32), jax.ShapeDtypeStruct((N, LANES), F32)],
        scratch_shapes=[pltpu.VMEM((NM, TB, LANES), F32)],
        compiler_params=_cparams(), name="merge_router",
    )(x, brf_ctx, brf_lat, brc, ys, gates, mod, w_glu, b_glu, w_ssm_out, w_out, norm2_g, rw_hi, rw_lo, rb)


def _routing_tables(top_i, top_p):
    e_flat = top_i[:, :TOPK].reshape(-1)
    p_flat = top_p[:, :TOPK].reshape(-1)
    na = N * TOPK
    key = e_flat * 32768 + jnp.arange(na, dtype=jnp.int32)
    skey, sw = lax.sort_key_val(key, p_flat)
    stok = (skey & 32767) // TOPK
    stok = jnp.concatenate([stok, jnp.full((TM,), N, jnp.int32)])
    sw = jnp.concatenate([sw, jnp.zeros((TM,), F32)])
    experts = jnp.arange(NE, dtype=jnp.int32)
    counts = jnp.sum((e_flat[:, None] == experts[None, :]).astype(jnp.int32), axis=0)
    ntile = (counts + TM - 1) // TM
    tend = jnp.cumsum(ntile)
    tstart = tend - ntile
    cstart = jnp.cumsum(counts) - counts
    n_tiles = tend[-1]
    t = jnp.minimum(jnp.arange(NT_MAX, dtype=jnp.int32), n_tiles - 1)
    tile_e = jnp.minimum(jnp.sum((t[:, None] >= tend[None, :]).astype(jnp.int32), axis=1), NE - 1)
    onehot = (tile_e[:, None] == experts[None, :]).astype(jnp.int32)
    pick = lambda tab: jnp.sum(onehot * tab[None, :], axis=1)
    k = t - pick(tstart)
    tile_j0 = pick(cstart) + k * TM
    tile_cnt = jnp.clip(pick(counts) - k * TM, 0, TM)
    return stok, sw, tile_e, tile_j0.astype(jnp.int32), tile_cnt.astype(jnp.int32), n_tiles.reshape(1)


def _gather_kernel(tok_ref, j0_ref, nt_ref, h_ref, o_ref, g_ref):
    i = pl.program_id(0)

    @pl.when(i < nt_ref[0])
    def _():
        j0 = j0_ref[i]

        def body(r, carry):
            tok = jnp.minimum(tok_ref[j0 + r], N - 1)
            g_ref[pl.ds(pl.multiple_of(r * DC, DC), DC), :] = h_ref[pl.ds(pl.multiple_of(tok * DC, DC), DC), :]
            return carry

        lax.fori_loop(0, TM, body, 0, unroll=8)
        for c in range(DC):
            o_ref[:, c * LANES:(c + 1) * LANES] = g_ref[pl.ds(c, TM, stride=DC), :].astype(BF16)

    @pl.when(i >= nt_ref[0])
    def _():
        o_ref[...] = jnp.zeros_like(o_ref)


def _gather(h_rows, stok, tile_j0, n_tiles):
    return pl.pallas_call(
        _gather_kernel,
        grid_spec=pltpu.PrefetchScalarGridSpec(
            num_scalar_prefetch=3, grid=(NT_MAX,),
            in_specs=[pl.BlockSpec((N * DC, LANES), lambda i, *_: (0, 0))],
            out_specs=pl.BlockSpec((TM, D), lambda i, *_: (i, 0)),
            scratch_shapes=[pltpu.VMEM((TM * DC, LANES), F32)]),
        out_shape=jax.ShapeDtypeStruct((R_MAX, D), BF16),
        compiler_params=_cparams(), name="moe_gather",
    )(stok, tile_j0, n_tiles, h_rows)


def _experts_kernel(tok_ref, w_ref, te_ref, j0_ref, cnt_ref, nt_ref,
                    x_ref, wgu_ref, bgu_ref, wdn_ref, bdn_ref, acc_ref, y_ref):
    i = pl.program_id(0)

    @pl.when(i == 0)
    def _():
        acc_ref[...] = jnp.zeros_like(acc_ref)

    @pl.when(i < nt_ref[0])
    def _():
        x = x_ref[...]
        y = jnp.broadcast_to(bdn_ref[...], (TM, D))
        for f in range(0, DFF, FC):
            gate = jnp.dot(x, wgu_ref[:, f:f + FC].astype(BF16), preferred_element_type=F32) + bgu_ref[:, f:f + FC]
            up = (jnp.dot(x, wgu_ref[:, DFF + f:DFF + f + FC].astype(BF16), preferred_element_type=F32)
                  + bgu_ref[:, DFF + f:DFF + f + FC])
            gate = jnp.minimum(gate, LIMIT)
            up = jnp.clip(up, -LIMIT, LIMIT)
            act = (up + 1.0) * (gate * _sigmoid(ALPHA * gate))
            y = y + jnp.dot(act.astype(BF16), wdn_ref[f:f + FC, :].astype(BF16), preferred_element_type=F32)
        _std_to_rows(y_ref, y, TM)
        j0 = j0_ref[i]
        cnt = cnt_ref[i]

        def body(b, carry):
            r0 = b * SCAT
            toks, ws = [], []
            for j in range(SCAT):
                real = r0 + j < cnt
                toks.append(jnp.where(real, tok_ref[j0 + r0 + j], N))
                ws.append(jnp.where(real, w_ref[j0 + r0 + j], 0.0))
            olds = [acc_ref[pl.ds(pl.multiple_of(t * DC, DC), DC), :] for t in toks]
            news = [olds[j] + ws[j] * y_ref[pl.ds(pl.multiple_of((r0 + j) * DC, DC), DC), :] for j in range(SCAT)]
            for j in range(SCAT):
                acc_ref[pl.ds(pl.multiple_of(toks[j] * DC, DC), DC), :] = news[j]
            return carry

        lax.fori_loop(0, (cnt + SCAT - 1) // SCAT, body, 0)


def _experts(layer, x_sorted, stok, sw, tile_e, tile_j0, tile_cnt, n_tiles, w_gate_up, b_gate_up, w_down, b_down):
    tile = lambda i, tok, w, te, j0, cnt, nt: (jnp.minimum(i, nt[0] - 1), 0)
    ex4 = lambda i, tok, w, te, j0, cnt, nt: (layer, te[i], 0, 0)
    return pl.pallas_call(
        _experts_kernel,
        grid_spec=pltpu.PrefetchScalarGridSpec(
            num_scalar_prefetch=6, grid=(NT_MAX,),
            in_specs=[pl.BlockSpec((TM, D), tile),
                      pl.BlockSpec((None, None, D, 2 * DFF), ex4),
                      pl.BlockSpec((None, None, 1, 2 * DFF), ex4),
                      pl.BlockSpec((None, None, DFF, D), ex4),
                      pl.BlockSpec((None, None, 1, D), ex4)],
            out_specs=pl.BlockSpec(((N + SUB) * DC, LANES), lambda i, *_: (0, 0)),
            scratch_shapes=[pltpu.VMEM((TM * DC, LANES), F32)]),
        out_shape=jax.ShapeDtypeStruct(((N + SUB) * DC, LANES), F32),
        compiler_params=_cparams(VMEM_MOE), name="moe_experts",
    )(stok, sw, tile_e, tile_j0, tile_cnt, n_tiles, x_sorted, w_gate_up,
      b_gate_up.reshape(DEPTH, NE, 1, 2 * DFF), w_down, b_down.reshape(DEPTH, NE, 1, D))


def _final_kernel(x_ref, moe_ref, mod_ref, g_ref, o_ref):
    x = x_ref[...] + mod_ref[:, 5 * D:6 * D] * _rows_to_std(moe_ref, TB)
    ms = jnp.mean(x * x, axis=-1, keepdims=True)
    o_ref[...] = (x * lax.rsqrt(ms + EPS)) * g_ref[...]


def _final(x, moe, mod, g):
    row = lambda i: (i, 0)
    return pl.pallas_call(
        _final_kernel,
        grid=(NB,),
        in_specs=[pl.BlockSpec((TB, D), row), pl.BlockSpec((TB * DC, LANES), row),
                  pl.BlockSpec((None, 1, NMOD * D), lambda i: (_mod_row(i), 0, 0)), _const_spec((1, D))],
        out_specs=pl.BlockSpec((TB, D), row),
        out_shape=jax.ShapeDtypeStruct((N, D), F32),
        compiler_params=_cparams(), name="final_norm",
    )(x, moe, mod, g)


def _grid_pos_embed(rows):
    quarter = D // 4
    freqs = jnp.exp(-math.log(10000.0) * jnp.arange(quarter, dtype=F32) / quarter)
    r = jnp.repeat(jnp.arange(rows, dtype=F32), GRID_W)
    col = jnp.tile(jnp.arange(GRID_W, dtype=F32), rows)
    ar = r[:, None] * freqs
    ac = col[:, None] * freqs
    return jnp.concatenate([jnp.sin(ar), jnp.cos(ar), jnp.sin(ac), jnp.cos(ac)], axis=-1)


def kernel(x_prompt, x_sample, state_ssm_re, state_ssm_im, c, c_ctx, w_mod, b_mod, norm1_g, norm2_g, w_in, b_in, w_four, conv_dw, conv_dw_b, conv_ln_g, conv_ln_b, w_conv_out, ssm_lam_re, ssm_lam_im, ssm_log_dt, ssm_b_re, ssm_b_im, ssm_c_re, ssm_c_im, ssm_d, w_ssm_glu, b_ssm_glu, w_ssm_out, w_out, router_w, router_b, w_gate_up, b_gate_up, w_down, b_down, final_norm_g):
    xs = x_sample + _grid_pos_embed(DEC_SEQ // GRID_W)[None]
    x = jnp.concatenate([x_prompt.reshape(N_CTX, D), xs.reshape(N_LAT, D)], axis=0)

    cond8 = jnp.concatenate([c_ctx[None, :], c, jnp.zeros((SUB - 1 - DEC_BATCH, D), F32)], axis=0)
    mod_all = _adaln(cond8, w_mod, b_mod)[:, :1 + DEC_BATCH].reshape(DEPTH, 1 + DEC_BATCH, 1, NMOD * D)

    w_in_b, w_four_b, w_conv_b = w_in.astype(BF16), w_four.astype(BF16), w_conv_out.astype(BF16)
    w_glu_b, w_sso_b, w_out_b = w_ssm_glu.astype(BF16), w_ssm_out.astype(BF16), w_out.astype(BF16)
    rw = jnp.pad(router_w, ((0, 0), (0, 0), (0, LANES - NE)))
    rw_hi = rw.astype(BF16)
    rw_lo = (rw - rw_hi.astype(F32)).astype(BF16)
    rb = jnp.pad(router_b, ((0, 0), (0, LANES - NE)), constant_values=-1e30).reshape(DEPTH, 1, LANES)
    r3 = lambda a: a.reshape(DEPTH, 1, a.shape[-1])

    m_all, ws_all, vs_all, a16_all = jax.vmap(_ssm_params)(
        ssm_lam_re, ssm_lam_im, ssm_log_dt, ssm_b_re, ssm_b_im, ssm_c_re, ssm_c_im, ssm_d)
    h0_all = jnp.stack([state_ssm_re[:, :, 0], state_ssm_im[:, :, 0], state_ssm_re[:, :, 1], state_ssm_im[:, :, 1]],
                       axis=0)
    h0_all = h0_all.reshape(4, DEC_BATCH, DEPTH, NPAIR, LANES).transpose(2, 0, 3, 1, 4)

    fins = []
    moe = None
    mod_prev = None
    for l in range(DEPTH):
        mod = mod_all[l]
        x, zf, v, u, gates = _inproj(l, x, moe, mod_prev, mod, r3(norm1_g), w_in_b, r3(b_in))
        brf_ctx = _fourier(l, zf, w_four_b, SEQ, 0, BATCH)
        brf_lat = _fourier(l, zf, w_four_b, DEC_SEQ, N_CTX // DEC_SEQ, DEC_BATCH)
        brc = _conv(l, v, conv_dw, r3(conv_dw_b), r3(conv_ln_g), r3(conv_ln_b), w_conv_b)
        ys, fin = _ssm(l, u, m_all, ws_all, vs_all, a16_all, h0_all)
        fins.append(fin)

        x, h_rows, top_i, top_p = _merge(l, x, brf_ctx, brf_lat, brc, ys, gates, mod, w_glu_b, r3(b_ssm_glu),
                                         w_sso_b, w_out_b, r3(norm2_g), rw_hi, rw_lo, rb)
        stok, sw, tile_e, tile_j0, tile_cnt, n_tiles = _routing_tables(top_i, top_p)
        x_sorted = _gather(h_rows, stok, tile_j0, n_tiles)
        moe = _experts(l, x_sorted, stok, sw, tile_e, tile_j0, tile_cnt, n_tiles,
                       w_gate_up, b_gate_up, w_down, b_down)
        mod_prev = mod

    y = _final(x, moe, mod_prev, final_norm_g.reshape(1, D))
    fin = jnp.stack(fins, axis=0).transpose(3, 0, 1, 2, 4).reshape(BATCH, DEPTH, 2, 2, SG, SP)
    new_re = fin[:, :, :, 0]
    new_im = fin[:, :, :, 1]
    return (y[:N_CTX].reshape(BATCH, SEQ, D), y[N_CTX:].reshape(DEC_BATCH, DEC_SEQ, D), new_re, new_im)
```

```python
import functools
import math

import numpy as np
import jax
import jax.numpy as jnp
from jax import lax
from jax.experimental import pallas as pl
from jax.experimental.pallas import tpu as pltpu

F32 = jnp.float32
BF16 = jnp.bfloat16
HI = lax.Precision.HIGHEST

D = 1024
BATCH, SEQ = 16, 256
DEC_BATCH, DEC_SEQ = 2, 1024
DEPTH = 4
N_CTX = BATCH * SEQ
N_LAT = DEC_BATCH * DEC_SEQ
N = N_CTX + N_LAT
GRID_W = 64
FW, FG = 512, 4
FGD = FW // FG
CC, TAPS = 512, 31
PAD = TAPS // 2
SW, SH = 512, 16
SG = SW // SH
SP = 64
IN_COLS = FW + 2 * CC + SW + 3 * D
NE, TOPK, DFF = 32, 4, 1024
LIMIT, ALPHA = 7.0, 1.702
EPS = 1e-6
NMOD = 6

TB = 256
NB = N // TB
CTX_BLOCKS = N_CTX // TB
LAT_BLOCKS_PER_SEQ = DEC_SEQ // TB
CH = 16
NCH = N // CH
NCH_CTX = N_CTX // CH
CPS_CTX = SEQ // CH
CPS_LAT = DEC_SEQ // CH
NPAIR = SG // 2
TM = 256
NT_MAX = (N * TOPK + NE * (TM - 1) + TM - 1) // TM
R_MAX = NT_MAX * TM
SUB = 8
LANES = 128
DC = D // LANES
SCAT = 16
FC = 512
VMEM_DEFAULT = 48 * 1024 * 1024
VMEM_MOE = 62 * 1024 * 1024


def _cparams(vmem=VMEM_DEFAULT, ndim=1):
    return pltpu.CompilerParams(dimension_semantics=("arbitrary",) * ndim, vmem_limit_bytes=vmem)


def _const_spec(shape, layer=None):
    if layer is None:
        zeros = (0,) * len(shape)
        return pl.BlockSpec(shape, lambda *_: zeros)
    zeros = (0,) * len(shape)
    return pl.BlockSpec((None,) + tuple(shape), lambda *_: (layer,) + zeros, pipeline_mode=pl.Buffered(1))


def _mod_row(i):
    return jnp.where(i < CTX_BLOCKS, 0, 1 + (i - CTX_BLOCKS) // LAT_BLOCKS_PER_SEQ)


def _sigmoid(x):
    return 1.0 / (1.0 + jnp.exp(-x))


def _rows_to_std(ref, rows):
    return jnp.concatenate([ref[pl.ds(c, rows, stride=DC), :] for c in range(DC)], axis=1)


def _std_to_rows(ref, val, rows):
    for c in range(DC):
        ref[pl.ds(c, rows, stride=DC), :] = val[:, c * LANES:(c + 1) * LANES]


NM = SW // LANES
GPV = LANES // SH
CPB = TB // CH


def _lane_group():
    return lax.broadcasted_iota(jnp.int32, (CPB, LANES), 1) // SH


def _tokens_to_chunks(u, slab_ref, out_ref):
    for m in range(NM):
        slab_ref[m] = u[:, m * LANES:(m + 1) * LANES]
    grp = _lane_group()
    for m in range(NM):
        for kk in range(CH // GPV):
            zs = [slab_ref[m, pl.ds(kk * GPV + k8, CPB, stride=CH), :] for k8 in range(GPV)]
            for r in range(GPV):
                acc = None
                for k8 in range(GPV):
                    sh = ((k8 - r) % GPV) * SH
                    piece = zs[k8] if sh == 0 else pltpu.roll(zs[k8], sh, 1)
                    acc = piece if acc is None else jnp.where(grp == k8, piece, acc)
                out_ref[m * GPV + r, :, kk * LANES:(kk + 1) * LANES] = acc.astype(out_ref.dtype)


def _chunks_to_tokens(y_ref, slab_ref):
    grp = _lane_group()
    for m in range(NM):
        for kk in range(CH // GPV):
            ys = [y_ref[m * GPV + r, :, kk * LANES:(kk + 1) * LANES] for r in range(GPV)]
            for t8 in range(GPV):
                acc = None
                for r in range(GPV):
                    sh = ((r - t8) % GPV) * SH
                    piece = ys[r] if sh == 0 else pltpu.roll(ys[r], sh, 1)
                    acc = piece if acc is None else jnp.where(grp == r, piece, acc)
                slab_ref[m, pl.ds(kk * GPV + t8, CPB, stride=CH), :] = acc
    return jnp.concatenate([slab_ref[m] for m in range(NM)], axis=1)


MOD_COLS = 1536


def _mod_kernel(cond_ref, w_ref, b_ref, o_ref):
    c = cond_ref[...]
    s = c * _sigmoid(c)
    o_ref[...] = jnp.dot(s.astype(BF16), w_ref[...].astype(BF16), preferred_element_type=F32) + b_ref[...]


def _adaln(cond8, w_mod, b_mod):
    nc = NMOD * D // MOD_COLS
    return pl.pallas_call(
        _mod_kernel,
        grid=(DEPTH, nc),
        in_specs=[
            pl.BlockSpec((SUB, D), lambda l, j: (0, 0)),
            pl.BlockSpec((None, D, MOD_COLS), lambda l, j: (l, 0, j)),
            pl.BlockSpec((None, 1, MOD_COLS), lambda l, j: (l, 0, j)),
        ],
        out_specs=pl.BlockSpec((None, SUB, MOD_COLS), lambda l, j: (l, 0, j)),
        out_shape=jax.ShapeDtypeStruct((DEPTH, SUB, NMOD * D), F32),
        compiler_params=_cparams(ndim=2),
        name="adaln_mod",
    )(cond8, w_mod, b_mod.reshape(DEPTH, 1, NMOD * D))


def _inproj_kernel(has_moe, *refs):
    if has_moe:
        (x_ref, moe_ref, modp_ref, mod_ref, g_ref, w_ref, b_ref,
         xo_ref, zf_ref, v_ref, u_ref, gate_ref, slab_ref) = refs
        x = x_ref[...] + modp_ref[:, 5 * D:6 * D] * _rows_to_std(moe_ref, TB)
        xo_ref[...] = x
    else:
        x_ref, mod_ref, g_ref, w_ref, b_ref, zf_ref, v_ref, u_ref, gate_ref, slab_ref = refs
        x = x_ref[...]
    sh1 = mod_ref[:, 0:D]
    sc1 = mod_ref[:, D:2 * D]
    ms = jnp.mean(x * x, axis=-1, keepdims=True)
    h = (x * lax.rsqrt(ms + EPS)) * g_ref[...] * (1.0 + sc1) + sh1
    z = jnp.dot(h.astype(BF16), w_ref[...], preferred_element_type=F32) + b_ref[...]
    o = 0
    zf_ref[...] = z[:, o:o + FW].astype(BF16)
    o += FW
    za = z[:, o:o + CC]
    zb = z[:, o + CC:o + 2 * CC]
    v_ref[...] = (za * _sigmoid(zb)).astype(BF16)
    o += 2 * CC
    _tokens_to_chunks(z[:, o:o + SW], slab_ref, u_ref)
    o += SW
    gate_ref[...] = _sigmoid(z[:, o:]).astype(BF16)


def _inproj(layer, x, moe, mod_prev, mod, norm_g, w_in, b_in):
    has_moe = moe is not None
    row = lambda i: (i, 0)
    mod_spec = pl.BlockSpec((None, 1, NMOD * D), lambda i: (_mod_row(i), 0, 0))
    in_specs = [pl.BlockSpec((TB, D), row)]
    args = [x]
    if has_moe:
        in_specs += [pl.BlockSpec((TB * DC, LANES), row), mod_spec]
        args += [moe, mod_prev]
    in_specs += [mod_spec, _const_spec((1, D), layer), _const_spec((D, IN_COLS), layer),
                 _const_spec((1, IN_COLS), layer)]
    args += [mod, norm_g, w_in, b_in]
    out_specs = [pl.BlockSpec((TB, FW), row), pl.BlockSpec((TB, CC), row),
                 pl.BlockSpec((SG, CPB, CH * SH), lambda i: (0, i, 0)), pl.BlockSpec((TB, 3 * D), row)]
    out_shape = [jax.ShapeDtypeStruct((N, FW), BF16), jax.ShapeDtypeStruct((N, CC), BF16),
                 jax.ShapeDtypeStruct((SG, NCH, CH * SH), BF16), jax.ShapeDtypeStruct((N, 3 * D), BF16)]
    if has_moe:
        out_specs = [pl.BlockSpec((TB, D), row)] + out_specs
        out_shape = [jax.ShapeDtypeStruct((N, D), F32)] + out_shape
    outs = pl.pallas_call(
        functools.partial(_inproj_kernel, has_moe),
        grid=(NB,), in_specs=in_specs, out_specs=out_specs, out_shape=out_shape,
        scratch_shapes=[pltpu.VMEM((NM, TB, LANES), F32)],
        compiler_params=_cparams(), name=f"inproj_{int(has_moe)}",
    )(*args)
    if has_moe:
        return outs
    return [x] + list(outs)


def _fourier_kernel(scale, zf_ref, csbd_ref, csl_ref, wf_ref, o_ref):
    t = jnp.dot(zf_ref[...], csbd_ref[...], preferred_element_type=F32).astype(BF16)
    ts = jnp.concatenate([t[:, :FW], t[:, FW:]], axis=0)
    y = jnp.dot(csl_ref[...], ts, preferred_element_type=F32) * scale
    o_ref[...] = jnp.dot(y.astype(BF16), wf_ref[...], preferred_element_type=F32).astype(BF16)


def _dft_consts(L):
    k = np.arange(L)
    ang = 2.0 * np.pi * ((k[:, None] * k[None, :]) % L) / L
    csl = np.concatenate([np.cos(ang), -np.sin(ang)], axis=1)
    m = np.arange(FGD)
    angc = 2.0 * np.pi * ((m[:, None] * m[None, :]) % FGD) / FGD
    cbd = np.kron(np.eye(FG), np.cos(angc))
    sbd = np.kron(np.eye(FG), np.sin(angc))
    csbd = np.concatenate([cbd, sbd], axis=1)
    return jnp.asarray(csl, F32).astype(BF16), jnp.asarray(csbd, F32).astype(BF16)


def _fourier(layer, zf, w_four, L, first_block, nseq):
    csl, csbd = _dft_consts(L)
    scale = 1.0 / math.sqrt(L * FGD)
    return pl.pallas_call(
        functools.partial(_fourier_kernel, scale),
        grid=(nseq,),
        in_specs=[pl.BlockSpec((L, FW), lambda i: (i + first_block, 0)),
                  _const_spec((FW, 2 * FW)), _const_spec((L, 2 * L)), _const_spec((FW, D), layer)],
        out_specs=pl.BlockSpec((L, D), lambda i: (i, 0)),
        out_shape=jax.ShapeDtypeStruct((nseq * L, D), BF16),
        compiler_params=_cparams(), name=f"fourier_{L}",
    )(zf, csbd, csl, w_four)


HALO = 16
CONV_ROWS = 64


def _conv_kernel(vp_ref, vc_ref, vn_ref, cw_ref, cb_ref, lg_ref, lb_ref, wc_ref, o_ref, pad_ref, act_ref):
    i = pl.program_id(0)
    pos = (i - CTX_BLOCKS) % LAT_BLOCKS_PER_SEQ
    lat = i >= CTX_BLOCKS
    has_prev = jnp.logical_and(lat, pos != 0)
    has_next = jnp.logical_and(lat, pos != LAT_BLOCKS_PER_SEQ - 1)
    prev = vp_ref[TB - HALO:, :].astype(F32)
    nxt = vn_ref[:HALO, :].astype(F32)
    pad_ref[0:HALO, :] = jnp.where(has_prev, prev, 0.0)
    pad_ref[HALO:HALO + TB, :] = vc_ref[...].astype(F32)
    pad_ref[HALO + TB:, :] = jnp.where(has_next, nxt, 0.0)
    for r0 in range(0, TB, CONV_ROWS):
        acc = jnp.broadcast_to(cb_ref[...], (CONV_ROWS, CC))
        for j in range(TAPS):
            s = r0 + HALO - PAD + j
            acc = acc + pad_ref[s:s + CONV_ROWS, :] * cw_ref[j:j + 1, :]
        mu = jnp.mean(acc, axis=-1, keepdims=True)
        xc = acc - mu
        var = jnp.mean(xc * xc, axis=-1, keepdims=True)
        y = xc * lax.rsqrt(var + EPS) * lg_ref[...] + lb_ref[...]
        act_ref[r0:r0 + CONV_ROWS, :] = (y * _sigmoid(y)).astype(BF16)
    o_ref[...] = jnp.dot(act_ref[...], wc_ref[...], preferred_element_type=F32).astype(BF16)


def _conv(layer, v, conv_dw, conv_b, ln_g, ln_b, w_conv_out):
    return pl.pallas_call(
        _conv_kernel,
        grid=(NB,),
        in_specs=[pl.BlockSpec((TB, CC), lambda i: (jnp.maximum(i - 1, 0), 0)),
                  pl.BlockSpec((TB, CC), lambda i: (i, 0)),
                  pl.BlockSpec((TB, CC), lambda i: (jnp.minimum(i + 1, NB - 1), 0)),
                  _const_spec((TAPS, CC), layer), _const_spec((1, CC), layer), _const_spec((1, CC), layer),
                  _const_spec((1, CC), layer), _const_spec((CC, D), layer)],
        out_specs=pl.BlockSpec((TB, D), lambda i: (i, 0)),
        out_shape=jax.ShapeDtypeStruct((N, D), BF16),
        scratch_shapes=[pltpu.VMEM((TB + 2 * HALO, CC), F32), pltpu.VMEM((TB, CC), BF16)],
        compiler_params=_cparams(), name="conv_branch",
    )(v, v, v, conv_dw, conv_b, ln_g, ln_b, w_conv_out)


def _ssm_params(lam_re, lam_im, log_dt, b_re, b_im, c_re, c_im, d):
    dt = jnp.exp(log_dt)[..., None]
    ldr, ldi = lam_re * dt, lam_im * dt
    j = jnp.arange(CH + 1, dtype=F32)
    mag = jnp.exp(ldr[..., None] * j)
    pr, pi = mag * jnp.cos(ldi[..., None] * j), mag * jnp.sin(ldi[..., None] * j)
    nr, ni = pr[..., 1] - 1.0, pi[..., 1]
    den = lam_re * lam_re + lam_im * lam_im
    qr, qi = (nr * lam_re + ni * lam_im) / den, (ni * lam_re - nr * lam_im) / den
    bbr = qr[..., None] * b_re - qi[..., None] * b_im
    bbi = qr[..., None] * b_im + qi[..., None] * b_re
    xr = pr[..., :CH, None] * bbr[..., None, :] - pi[..., :CH, None] * bbi[..., None, :]
    xi = pr[..., :CH, None] * bbi[..., None, :] + pi[..., :CH, None] * bbr[..., None, :]
    kk = (jnp.einsum("dghp,dgpji->dgjhi", c_re, xr, precision=HI)
          - jnp.einsum("dghp,dgpji->dgjhi", c_im, xi, precision=HI))
    lag = np.arange(CH)[:, None, None]
    ki = np.arange(CH)[None, :, None]
    ti = np.arange(CH)[None, None, :]
    self_f = (ti - ki == lag).astype(np.float32)
    self_b = (ki - ti == lag).astype(np.float32)
    mm = (jnp.einsum("gjhi,jkt->gkith", kk[0], self_f, precision=HI)
          + jnp.einsum("gjhi,jkt->gkith", kk[1], self_b, precision=HI))
    eye_t = np.eye(CH, dtype=np.float32)
    eye_h = np.eye(SH, dtype=np.float32)
    dd = d.reshape(SG, SH)
    mm = mm + dd[:, None, :, None, None] * eye_t[None, :, None, :, None] * eye_h[None, None, :, None, :]
    m = mm.reshape(SG, CH * SH, CH * SH)
    wfr = jnp.transpose(xr[0][:, :, ::-1, :], (0, 2, 3, 1)).reshape(SG, CH * SH, SP)
    wfi = jnp.transpose(xi[0][:, :, ::-1, :], (0, 2, 3, 1)).reshape(SG, CH * SH, SP)
    wbr = jnp.transpose(xr[1], (0, 2, 3, 1)).reshape(SG, CH * SH, SP)
    wbi = jnp.transpose(xi[1], (0, 2, 3, 1)).reshape(SG, CH * SH, SP)
    ctr, cti = jnp.transpose(c_re, (0, 1, 3, 2)), jnp.transpose(c_im, (0, 1, 3, 2))
    pfr, pfi = pr[0][..., 1:], pi[0][..., 1:]
    pbr, pbi = pr[1][..., 1:][..., ::-1], pi[1][..., 1:][..., ::-1]
    afr = ctr[0][:, :, None, :] * pfr[..., None] - cti[0][:, :, None, :] * pfi[..., None]
    afi = ctr[0][:, :, None, :] * pfi[..., None] + cti[0][:, :, None, :] * pfr[..., None]
    abr = ctr[1][:, :, None, :] * pbr[..., None] - cti[1][:, :, None, :] * pbi[..., None]
    abi = ctr[1][:, :, None, :] * pbi[..., None] + cti[1][:, :, None, :] * pbr[..., None]
    kinds_w = [wfr, wfi, wbr, wbi]
    kinds_v = [a.reshape(SG, SP, CH * SH) for a in (afr, -afi, abr, -abi)]
    eye_q = np.eye(2, dtype=np.float32)
    w4 = jnp.stack(kinds_w, axis=2).reshape(NPAIR, 2, CH * SH, 4, 1, SP)
    ws = (w4 * eye_q[None, :, None, None, :, None]).reshape(NPAIR, 2, CH * SH, 4 * LANES)
    v4 = jnp.stack(kinds_v, axis=1).reshape(NPAIR, 2, 4, 1, SP, CH * SH)
    vs = (v4 * eye_q[None, :, None, :, None, None]).reshape(NPAIR, 2, 4 * LANES, CH * SH)
    a16 = jnp.stack([pr[0][..., CH], pi[0][..., CH], pr[1][..., CH], pi[1][..., CH]], 0)
    a16 = a16.reshape(4, NPAIR, 1, LANES)
    return m.astype(BF16), ws.astype(BF16), vs.astype(BF16), a16


def _ssm_kernel(u_ref, m_ref, ws_ref, vs_ref, a_ref, h0_ref, y_ref, fin_ref, s_ref, ef_ref):
    u0, u1 = u_ref[0], u_ref[1]
    s = (jnp.dot(u0, ws_ref[0], preferred_element_type=F32)
         + jnp.dot(u1, ws_ref[1], preferred_element_type=F32))
    for k in range(4):
        s_ref[k] = s[:, k * LANES:(k + 1) * LANES]

    def scan(kr, ki, first, nrows, nsteps, reverse, er, ei):
        ar, ai = a_ref[kr], a_ref[ki]
        order = range(nsteps - 1, -1, -1) if reverse else range(nsteps)
        for c in order:
            rows = pl.ds(first + c, nrows, stride=nsteps)
            ef_ref[kr, rows, :] = er
            ef_ref[ki, rows, :] = ei
            sr, si = s_ref[kr, rows, :], s_ref[ki, rows, :]
            er, ei = ar * er - ai * ei + sr, ar * ei + ai * er + si
        return er, ei

    zc = jnp.zeros((BATCH, LANES), F32)
    for kr, ki, rev in ((0, 1, False), (2, 3, True)):
        er, ei = scan(kr, ki, 0, BATCH, CPS_CTX, rev, zc, zc)
        fin_ref[kr] = er
        fin_ref[ki] = ei
        scan(kr, ki, NCH_CTX, DEC_BATCH, CPS_LAT, rev, h0_ref[kr], h0_ref[ki])

    ef = jnp.concatenate([ef_ref[k] for k in range(4)], axis=1).astype(BF16)
    for q, uq in ((0, u0), (1, u1)):
        y_ref[q] = (jnp.dot(uq, m_ref[q], preferred_element_type=F32)
                    + jnp.dot(ef, vs_ref[q], preferred_element_type=F32))


def _ssm(layer, u_chunks, m, ws, vs, a16, h0):
    pair = lambda j: (j, 0, 0)
    return pl.pallas_call(
        _ssm_kernel,
        grid=(NPAIR,),
        in_specs=[pl.BlockSpec((2, NCH, CH * SH), pair),
                  pl.BlockSpec((None, 2, CH * SH, CH * SH), lambda j: (layer, j, 0, 0)),
                  pl.BlockSpec((None, None, 2, CH * SH, 4 * LANES), lambda j: (layer, j, 0, 0, 0)),
                  pl.BlockSpec((None, None, 2, 4 * LANES, CH * SH), lambda j: (layer, j, 0, 0, 0)),
                  pl.BlockSpec((None, 4, None, 1, LANES), lambda j: (layer, 0, j, 0, 0)),
                  pl.BlockSpec((None, 4, None, DEC_BATCH, LANES), lambda j: (layer, 0, j, 0, 0))],
        out_specs=[pl.BlockSpec((2, NCH, CH * SH), pair),
                   pl.BlockSpec((4, None, BATCH, LANES), lambda j: (0, j, 0, 0))],
        out_shape=[jax.ShapeDtypeStruct((SG, NCH, CH * SH), F32),
                   jax.ShapeDtypeStruct((4, NPAIR, BATCH, LANES), F32)],
        scratch_shapes=[pltpu.VMEM((4, NCH, LANES), F32), pltpu.VMEM((4, NCH, LANES), F32)],
        compiler_params=_cparams(), name="s5_chunked",
    )(u_chunks, m, ws, vs, a16, h0)


def _merge_kernel(x_ref, bfc_ref, bfl_ref, bc_ref, ys_ref, gate_ref, mod_ref, wg_ref, bg_ref, wso_ref, wo_ref,
                  n2_ref, rwh_ref, rwl_ref, rb_ref, xo_ref, h_ref, ti_ref, tp_ref, slab_ref):
    i = pl.program_id(0)
    g = jax.nn.gelu(_chunks_to_tokens(ys_ref, slab_ref))
    gl = jnp.dot(g.astype(BF16), wg_ref[...], preferred_element_type=F32) + bg_ref[...]
    y2 = g * _sigmoid(gl)
    brs = jnp.dot(y2.astype(BF16), wso_ref[...], preferred_element_type=F32)
    brf = jnp.where(i < CTX_BLOCKS, bfc_ref[...], bfl_ref[...]).astype(F32)
    mixed = (gate_ref[:, 0:D].astype(F32) * brf
             + gate_ref[:, D:2 * D].astype(F32) * bc_ref[...].astype(F32)
             + gate_ref[:, 2 * D:3 * D].astype(F32) * brs)
    g1 = mod_ref[:, 2 * D:3 * D]
    x = x_ref[...] + g1 * jnp.dot(mixed.astype(BF16), wo_ref[...], preferred_element_type=F32)
    xo_ref[...] = x
    sh2 = mod_ref[:, 3 * D:4 * D]
    sc2 = mod_ref[:, 4 * D:5 * D]
    ms = jnp.mean(x * x, axis=-1, keepdims=True)
    h = (x * lax.rsqrt(ms + EPS)) * n2_ref[...] * (1.0 + sc2) + sh2
    _std_to_rows(h_ref, h, TB)
    hh = h.astype(BF16)
    hl = (h - hh.astype(F32)).astype(BF16)
    logits = (jnp.dot(hh, rwh_ref[...], preferred_element_type=F32)
              + jnp.dot(hl, rwh_ref[...], preferred_element_type=F32)
              + jnp.dot(hh, rwl_ref[...], preferred_element_type=F32)) + rb_ref[...]
    lane = lax.broadcasted_iota(jnp.int32, (TB, LANES), 1).astype(F32)
    vals, idxs = [], []
    l = logits
    for _ in range(TOPK):
        m = jnp.max(l, axis=-1, keepdims=True)
        idx = jnp.min(jnp.where(l == m, lane, float(LANES)), axis=-1, keepdims=True)
        vals.append(m)
        idxs.append(idx)
        l = jnp.where(lane == idx, -jnp.inf, l)
    es = [jnp.exp(v - vals[0]) for v in vals]
    tot = es[0] + es[1] + es[2] + es[3]
    ti = jnp.zeros((TB, LANES), F32)
    tp = jnp.zeros((TB, LANES), F32)
    for k in range(TOPK):
        ti = jnp.where(lane == float(k), idxs[k], ti)
        tp = jnp.where(lane == float(k), es[k] / tot, tp)
    ti_ref[...] = ti.astype(jnp.int32)
    tp_ref[...] = tp


def _merge(layer, x, brf_ctx, brf_lat, brc, ys, gates, mod, w_glu, b_glu, w_ssm_out, w_out, norm2_g,
           rw_hi, rw_lo, rb):
    row = lambda i: (i, 0)
    return pl.pallas_call(
        _merge_kernel,
        grid=(NB,),
        in_specs=[pl.BlockSpec((TB, D), row),
                  pl.BlockSpec((TB, D), lambda i: (jnp.minimum(i, CTX_BLOCKS - 1), 0)),
                  pl.BlockSpec((TB, D), lambda i: (jnp.maximum(i - CTX_BLOCKS, 0), 0)),
                  pl.BlockSpec((TB, D), row),
                  pl.BlockSpec((SG, CPB, CH * SH), lambda i: (0, i, 0)),
                  pl.BlockSpec((TB, 3 * D), row),
                  pl.BlockSpec((None, 1, NMOD * D), lambda i: (_mod_row(i), 0, 0)),
                  _const_spec((SW, SW), layer), _const_spec((1, SW), layer), _const_spec((SW, D), layer),
                  _const_spec((D, D), layer), _const_spec((1, D), layer),
                  _const_spec((D, LANES), layer), _const_spec((D, LANES), layer), _const_spec((1, LANES), layer)],
        out_specs=[pl.BlockSpec((TB, D), row), pl.BlockSpec((TB * DC, LANES), row),
                   pl.BlockSpec((TB, LANES), row), pl.BlockSpec((TB, LANES), row)],
        out_shape=[jax.ShapeDtypeStruct((N, D), F32), jax.ShapeDtypeStruct((N * DC, LANES), F32),
                   jax.ShapeDtypeStruct((N, LANES), jnp.int32), jax.ShapeDtypeStruct((N, LANES), F32)],
        scratch_shapes=[pltpu.VMEM((NM, TB, LANES), F32)],
        compiler_params=_cparams(), name="merge_router",
    )(x, brf_ctx, brf_lat, brc, ys, gates, mod, w_glu, b_glu, w_ssm_out, w_out, norm2_g, rw_hi, rw_lo, rb)


def _routing_tables(top_i, top_p):
    e_flat = top_i[:, :TOPK].reshape(-1)
    p_flat = top_p[:, :TOPK].reshape(-1)
    na = N * TOPK
    key = e_flat * 32768 + jnp.arange(na, dtype=jnp.int32)
    skey, sw = lax.sort_key_val(key, p_flat)
    tok8 = ((skey & 32767) // TOPK) * DC
    tok8 = jnp.concatenate([tok8, jnp.zeros((TM,), jnp.int32)])
    sw = jnp.concatenate([sw, jnp.zeros((TM,), F32)])
    experts = jnp.arange(NE, dtype=jnp.int32)
    counts = jnp.sum((e_flat[:, None] == experts[None, :]).astype(jnp.int32), axis=0)
    ntile = (counts + TM - 1) // TM
    tend = jnp.cumsum(ntile)
    tstart = tend - ntile
    cstart = jnp.cumsum(counts) - counts
    n_tiles = tend[-1]
    later = jnp.logical_and(experts[None, :] > experts[:, None], ntile[None, :] > 0)
    nxt_e = jnp.min(jnp.where(later, experts[None, :], NE), axis=1)
    nxt_e = jnp.where(nxt_e == NE, -1, nxt_e)
    used = (ntile > 0).astype(jnp.int32)
    used_before = jnp.cumsum(used) - used
    ti = jnp.arange(NT_MAX, dtype=jnp.int32)
    t = jnp.minimum(ti, n_tiles - 1)
    tile_e = jnp.minimum(jnp.sum((t[:, None] >= tend[None, :]).astype(jnp.int32), axis=1), NE - 1)
    onehot = (tile_e[:, None] == experts[None, :]).astype(jnp.int32)
    pick = lambda tab: jnp.sum(onehot * tab[None, :], axis=1)
    k = t - pick(tstart)
    tile_j0 = pick(cstart) + k * TM
    tile_cnt = jnp.clip(pick(counts) - k * TM, 0, TM)
    tile_first = jnp.logical_and(k == 0, ti < n_tiles)
    tile_slot = pick(used_before) % 2
    tile_next = pick(nxt_e)
    i32 = lambda a: a.astype(jnp.int32)
    return (tok8, sw, i32(tile_e), i32(tile_j0), i32(tile_cnt), i32(tile_first), i32(tile_slot), i32(tile_next),
            i32(n_tiles.reshape(1)))


GROWS = 16


def _gather_kernel(tok8_ref, j0_ref, nt_ref, h_ref, o_ref, g_ref):
    i = pl.program_id(0)

    @pl.when(i < nt_ref[0])
    def _():
        j0 = j0_ref[i]

        def body(b, carry):
            base = j0 + b * GROWS
            rows = [h_ref[pl.ds(pl.multiple_of(tok8_ref[base + j], DC), DC), :] for j in range(GROWS)]
            for j in range(GROWS):
                g_ref[pl.ds(pl.multiple_of((b * GROWS + j) * DC, DC), DC), :] = rows[j]
            return carry

        lax.fori_loop(0, TM // GROWS, body, 0)
        for c in range(DC):
            o_ref[:, c * LANES:(c + 1) * LANES] = g_ref[pl.ds(c, TM, stride=DC), :].astype(BF16)

    @pl.when(i >= nt_ref[0])
    def _():
        o_ref[...] = jnp.zeros_like(o_ref)


def _gather(h_rows, tok8, tile_j0, n_tiles):
    return pl.pallas_call(
        _gather_kernel,
        grid_spec=pltpu.PrefetchScalarGridSpec(
            num_scalar_prefetch=3, grid=(NT_MAX,),
            in_specs=[pl.BlockSpec((N * DC, LANES), lambda i, *_: (0, 0))],
            out_specs=pl.BlockSpec((TM, D), lambda i, *_: (i, 0)),
            scratch_shapes=[pltpu.VMEM((TM * DC, LANES), F32)]),
        out_shape=jax.ShapeDtypeStruct((R_MAX, D), BF16),
        compiler_params=_cparams(), name="moe_gather",
    )(tok8, tile_j0, n_tiles, h_rows)


def _experts_kernel(layer, tok8_ref, w_ref, te_ref, j0_ref, cnt_ref, first_ref, slot_ref, next_ref, nt_ref,
                    x_ref, wgu_hbm, bgu_ref, wdn_hbm, bdn_ref, acc_ref, y_ref, wgu_buf, wdn_buf, sem):
    i = pl.program_id(0)

    def weight_copies(e, slot):
        return (pltpu.make_async_copy(wgu_hbm.at[layer, e], wgu_buf.at[slot], sem.at[0, slot]),
                pltpu.make_async_copy(wdn_hbm.at[layer, e], wdn_buf.at[slot], sem.at[1, slot]))

    @pl.when(i == 0)
    def _():
        for cp in weight_copies(te_ref[0], 0):
            cp.start()
        acc_ref[...] = jnp.zeros_like(acc_ref)

    live = i < nt_ref[0]
    slot = slot_ref[i]

    @pl.when(jnp.logical_and(live, first_ref[i] == 1))
    def _():
        for cp in weight_copies(te_ref[i], slot):
            cp.wait()

        @pl.when(next_ref[i] >= 0)
        def _():
            for cp in weight_copies(next_ref[i], 1 - slot):
                cp.start()

    @pl.when(live)
    def _():
        x = x_ref[...]
        y = jnp.broadcast_to(bdn_ref[...], (TM, D))
        for f in range(0, DFF, FC):
            wg = wgu_buf[slot, :, f:f + FC].astype(BF16)
            wu = wgu_buf[slot, :, DFF + f:DFF + f + FC].astype(BF16)
            gate = jnp.dot(x, wg, preferred_element_type=F32) + bgu_ref[:, f:f + FC]
            up = jnp.dot(x, wu, preferred_element_type=F32) + bgu_ref[:, DFF + f:DFF + f + FC]
            gate = jnp.minimum(gate, LIMIT)
            up = jnp.clip(up, -LIMIT, LIMIT)
            act = (up + 1.0) * (gate * _sigmoid(ALPHA * gate))
            y = y + jnp.dot(act.astype(BF16), wdn_buf[slot, f:f + FC, :].astype(BF16), preferred_element_type=F32)
        _std_to_rows(y_ref, y, TM)
        j0 = j0_ref[i]
        cnt = cnt_ref[i]

        def scatter(r0, masked):
            base = j0 + r0
            offs, ws = [], []
            for j in range(SCAT):
                o, w = tok8_ref[base + j], w_ref[base + j]
                if masked:
                    real = r0 + j < cnt
                    o, w = jnp.where(real, o, N * DC), jnp.where(real, w, 0.0)
                offs.append(pl.multiple_of(o, DC))
                ws.append(w)
            olds = [acc_ref[pl.ds(o, DC), :] for o in offs]
            news = [olds[j] + ws[j] * y_ref[pl.ds(pl.multiple_of((r0 + j) * DC, DC), DC), :] for j in range(SCAT)]
            for j in range(SCAT):
                acc_ref[pl.ds(offs[j], DC), :] = news[j]

        nfull = cnt // SCAT

        def body(b, carry):
            scatter(b * SCAT, False)
            return carry

        lax.fori_loop(0, nfull, body, 0)

        @pl.when(nfull * SCAT < cnt)
        def _():
            scatter(nfull * SCAT, True)


def _experts(layer, x_sorted, tables, w_gate_up, b_gate_up, w_down, b_down):
    tile = lambda i, *s: (jnp.minimum(i, s[-1][0] - 1), 0)
    bias = lambda i, *s: (layer, s[2][i], 0, 0)
    return pl.pallas_call(
        functools.partial(_experts_kernel, layer),
        grid_spec=pltpu.PrefetchScalarGridSpec(
            num_scalar_prefetch=len(tables), grid=(NT_MAX,),
            in_specs=[pl.BlockSpec((TM, D), tile),
                      pl.BlockSpec(memory_space=pl.ANY),
                      pl.BlockSpec((None, None, 1, 2 * DFF), bias),
                      pl.BlockSpec(memory_space=pl.ANY),
                      pl.BlockSpec((None, None, 1, D), bias)],
            out_specs=pl.BlockSpec(((N + SUB) * DC, LANES), lambda i, *_: (0, 0)),
            scratch_shapes=[pltpu.VMEM((TM * DC, LANES), F32),
                            pltpu.VMEM((2, D, 2 * DFF), F32), pltpu.VMEM((2, DFF, D), F32),
                            pltpu.SemaphoreType.DMA((2, 2))]),
        out_shape=jax.ShapeDtypeStruct(((N + SUB) * DC, LANES), F32),
        compiler_params=_cparams(VMEM_MOE), name="moe_experts",
    )(*tables, x_sorted, w_gate_up, b_gate_up.reshape(DEPTH, NE, 1, 2 * DFF), w_down,
      b_down.reshape(DEPTH, NE, 1, D))


def _final_kernel(x_ref, moe_ref, mod_ref, g_ref, o_ref):
    x = x_ref[...] + mod_ref[:, 5 * D:6 * D] * _rows_to_std(moe_ref, TB)
    ms = jnp.mean(x * x, axis=-1, keepdims=True)
    o_ref[...] = (x * lax.rsqrt(ms + EPS)) * g_ref[...]


def _final(x, moe, mod, g):
    row = lambda i: (i, 0)
    return pl.pallas_call(
        _final_kernel,
        grid=(NB,),
        in_specs=[pl.BlockSpec((TB, D), row), pl.BlockSpec((TB * DC, LANES), row),
                  pl.BlockSpec((None, 1, NMOD * D), lambda i: (_mod_row(i), 0, 0)), _const_spec((1, D))],
        out_specs=pl.BlockSpec((TB, D), row),
        out_shape=jax.ShapeDtypeStruct((N, D), F32),
        compiler_params=_cparams(), name="final_norm",
    )(x, moe, mod, g)


def _grid_pos_embed(rows):
    quarter = D // 4
    freqs = jnp.exp(-math.log(10000.0) * jnp.arange(quarter, dtype=F32) / quarter)
    r = jnp.repeat(jnp.arange(rows, dtype=F32), GRID_W)
    col = jnp.tile(jnp.arange(GRID_W, dtype=F32), rows)
    ar = r[:, None] * freqs
    ac = col[:, None] * freqs
    return jnp.concatenate([jnp.sin(ar), jnp.cos(ar), jnp.sin(ac), jnp.cos(ac)], axis=-1)


def kernel(x_prompt, x_sample, state_ssm_re, state_ssm_im, c, c_ctx, w_mod, b_mod, norm1_g, norm2_g, w_in, b_in, w_four, conv_dw, conv_dw_b, conv_ln_g, conv_ln_b, w_conv_out, ssm_lam_re, ssm_lam_im, ssm_log_dt, ssm_b_re, ssm_b_im, ssm_c_re, ssm_c_im, ssm_d, w_ssm_glu, b_ssm_glu, w_ssm_out, w_out, router_w, router_b, w_gate_up, b_gate_up, w_down, b_down, final_norm_g):
    xs = x_sample + _grid_pos_embed(DEC_SEQ // GRID_W)[None]
    x = jnp.concatenate([x_prompt.reshape(N_CTX, D), xs.reshape(N_LAT, D)], axis=0)

    cond8 = jnp.concatenate([c_ctx[None, :], c, jnp.zeros((SUB - 1 - DEC_BATCH, D), F32)], axis=0)
    mod_all = _adaln(cond8, w_mod, b_mod)[:, :1 + DEC_BATCH].reshape(DEPTH, 1 + DEC_BATCH, 1, NMOD * D)

    w_in_b, w_four_b, w_conv_b = w_in.astype(BF16), w_four.astype(BF16), w_conv_out.astype(BF16)
    w_glu_b, w_sso_b, w_out_b = w_ssm_glu.astype(BF16), w_ssm_out.astype(BF16), w_out.astype(BF16)
    rw = jnp.pad(router_w, ((0, 0), (0, 0), (0, LANES - NE)))
    rw_hi = rw.astype(BF16)
    rw_lo = (rw - rw_hi.astype(F32)).astype(BF16)
    rb = jnp.pad(router_b, ((0, 0), (0, LANES - NE)), constant_values=-1e30).reshape(DEPTH, 1, LANES)
    r3 = lambda a: a.reshape(DEPTH, 1, a.shape[-1])

    m_all, ws_all, vs_all, a16_all = jax.vmap(_ssm_params)(
        ssm_lam_re, ssm_lam_im, ssm_log_dt, ssm_b_re, ssm_b_im, ssm_c_re, ssm_c_im, ssm_d)
    h0_all = jnp.stack([state_ssm_re[:, :, 0], state_ssm_im[:, :, 0], state_ssm_re[:, :, 1], state_ssm_im[:, :, 1]],
                       axis=0)
    h0_all = h0_all.reshape(4, DEC_BATCH, DEPTH, NPAIR, LANES).transpose(2, 0, 3, 1, 4)

    fins = []
    moe = None
    mod_prev = None
    for l in range(DEPTH):
        mod = mod_all[l]
        x, zf, v, u, gates = _inproj(l, x, moe, mod_prev, mod, r3(norm1_g), w_in_b, r3(b_in))
        brf_ctx = _fourier(l, zf, w_four_b, SEQ, 0, BATCH)
        brf_lat = _fourier(l, zf, w_four_b, DEC_SEQ, N_CTX // DEC_SEQ, DEC_BATCH)
        brc = _conv(l, v, conv_dw, r3(conv_dw_b), r3(conv_ln_g), r3(conv_ln_b), w_conv_b)
        ys, fin = _ssm(l, u, m_all, ws_all, vs_all, a16_all, h0_all)
        fins.append(fin)

        x, h_rows, top_i, top_p = _merge(l, x, brf_ctx, brf_lat, brc, ys, gates, mod, w_glu_b, r3(b_ssm_glu),
                                         w_sso_b, w_out_b, r3(norm2_g), rw_hi, rw_lo, rb)
        tables = _routing_tables(top_i, top_p)
        x_sorted = _gather(h_rows, tables[0], tables[3], tables[-1])
        moe = _experts(l, x_sorted, tables, w_gate_up, b_gate_up, w_down, b_down)
        mod_prev = mod

    y = _final(x, moe, mod_prev, final_norm_g.reshape(1, D))
    fin = jnp.stack(fins, axis=0).transpose(3, 0, 1, 2, 4).reshape(BATCH, DEPTH, 2, 2, SG, SP)
    new_re = fin[:, :, :, 0]
    new_im = fin[:, :, :, 1]
    return (y[:N_CTX].reshape(BATCH, SEQ, D), y[N_CTX:].reshape(DEC_BATCH, DEC_SEQ, D), new_re, new_im)
```

```python
import functools
import math

import numpy as np
import jax
import jax.numpy as jnp
from jax import lax
from jax.experimental import pallas as pl
from jax.experimental.pallas import tpu as pltpu

F32 = jnp.float32
BF16 = jnp.bfloat16
HI = lax.Precision.HIGHEST

D = 1024
BATCH, SEQ = 16, 256
DEC_BATCH, DEC_SEQ = 2, 1024
DEPTH = 4
N_CTX = BATCH * SEQ
N_LAT = DEC_BATCH * DEC_SEQ
N = N_CTX + N_LAT
GRID_W = 64
FW, FG = 512, 4
FGD = FW // FG
CC, TAPS = 512, 31
PAD = TAPS // 2
SW, SH = 512, 16
SG = SW // SH
SP = 64
IN_COLS = FW + 2 * CC + SW + 3 * D
NE, TOPK, DFF = 32, 4, 1024
LIMIT, ALPHA = 7.0, 1.702
EPS = 1e-6
NMOD = 6

TB = 256
NB = N // TB
CTX_BLOCKS = N_CTX // TB
LAT_BLOCKS_PER_SEQ = DEC_SEQ // TB
CH = 16
NCH = N // CH
NCH_CTX = N_CTX // CH
CPS_CTX = SEQ // CH
CPS_LAT = DEC_SEQ // CH
NPAIR = SG // 2
TM = 256
NT_MAX = (N * TOPK + NE * (TM - 1) + TM - 1) // TM
R_MAX = NT_MAX * TM
SUB = 8
LANES = 128
DC = D // LANES
SCAT = 16
FC = 512
VMEM_DEFAULT = 48 * 1024 * 1024
VMEM_MOE = 62 * 1024 * 1024


def _cparams(vmem=VMEM_DEFAULT, ndim=1):
    return pltpu.CompilerParams(dimension_semantics=("arbitrary",) * ndim, vmem_limit_bytes=vmem)


def _const_spec(shape, layer=None):
    if layer is None:
        zeros = (0,) * len(shape)
        return pl.BlockSpec(shape, lambda *_: zeros)
    zeros = (0,) * len(shape)
    return pl.BlockSpec((None,) + tuple(shape), lambda *_: (layer,) + zeros, pipeline_mode=pl.Buffered(1))


def _mod_row(i):
    return jnp.where(i < CTX_BLOCKS, 0, 1 + (i - CTX_BLOCKS) // LAT_BLOCKS_PER_SEQ)


def _sigmoid(x):
    return 1.0 / (1.0 + jnp.exp(-x))


def _rows_to_std(ref, rows):
    return jnp.concatenate([ref[pl.ds(c, rows, stride=DC), :] for c in range(DC)], axis=1)


def _std_to_rows(ref, val, rows):
    for c in range(DC):
        ref[pl.ds(c, rows, stride=DC), :] = val[:, c * LANES:(c + 1) * LANES]


NM = SW // LANES
GPV = LANES // SH
CPB = TB // CH


def _lane_group():
    return lax.broadcasted_iota(jnp.int32, (CPB, LANES), 1) // SH


def _tokens_to_chunks(u, slab_ref, out_ref):
    for m in range(NM):
        slab_ref[m] = u[:, m * LANES:(m + 1) * LANES]
    grp = _lane_group()
    for m in range(NM):
        for kk in range(CH // GPV):
            zs = [slab_ref[m, pl.ds(kk * GPV + k8, CPB, stride=CH), :] for k8 in range(GPV)]
            for r in range(GPV):
                acc = None
                for k8 in range(GPV):
                    sh = ((k8 - r) % GPV) * SH
                    piece = zs[k8] if sh == 0 else pltpu.roll(zs[k8], sh, 1)
                    acc = piece if acc is None else jnp.where(grp == k8, piece, acc)
                out_ref[m * GPV + r, :, kk * LANES:(kk + 1) * LANES] = acc.astype(out_ref.dtype)


def _chunks_to_tokens(y_ref, slab_ref):
    grp = _lane_group()
    for m in range(NM):
        for kk in range(CH // GPV):
            ys = [y_ref[m * GPV + r, :, kk * LANES:(kk + 1) * LANES] for r in range(GPV)]
            for t8 in range(GPV):
                acc = None
                for r in range(GPV):
                    sh = ((r - t8) % GPV) * SH
                    piece = ys[r] if sh == 0 else pltpu.roll(ys[r], sh, 1)
                    acc = piece if acc is None else jnp.where(grp == r, piece, acc)
                slab_ref[m, pl.ds(kk * GPV + t8, CPB, stride=CH), :] = acc
    return jnp.concatenate([slab_ref[m] for m in range(NM)], axis=1)


MOD_COLS = 1536


def _mod_kernel(cond_ref, w_ref, b_ref, o_ref):
    c = cond_ref[...]
    s = c * _sigmoid(c)
    o_ref[...] = jnp.dot(s.astype(BF16), w_ref[...].astype(BF16), preferred_element_type=F32) + b_ref[...]


def _adaln(cond8, w_mod, b_mod):
    nc = NMOD * D // MOD_COLS
    return pl.pallas_call(
        _mod_kernel,
        grid=(DEPTH, nc),
        in_specs=[
            pl.BlockSpec((SUB, D), lambda l, j: (0, 0)),
            pl.BlockSpec((None, D, MOD_COLS), lambda l, j: (l, 0, j)),
            pl.BlockSpec((None, 1, MOD_COLS), lambda l, j: (l, 0, j)),
        ],
        out_specs=pl.BlockSpec((None, SUB, MOD_COLS), lambda l, j: (l, 0, j)),
        out_shape=jax.ShapeDtypeStruct((DEPTH, SUB, NMOD * D), F32),
        compiler_params=_cparams(ndim=2),
        name="adaln_mod",
    )(cond8, w_mod, b_mod.reshape(DEPTH, 1, NMOD * D))


def _inproj_kernel(has_moe, *refs):
    if has_moe:
        (x_ref, moe_ref, modp_ref, mod_ref, g_ref, w_ref, b_ref,
         xo_ref, zf_ref, v_ref, u_ref, gate_ref, slab_ref) = refs
        x = x_ref[...] + modp_ref[:, 5 * D:6 * D] * _rows_to_std(moe_ref, TB)
        xo_ref[...] = x
    else:
        x_ref, mod_ref, g_ref, w_ref, b_ref, zf_ref, v_ref, u_ref, gate_ref, slab_ref = refs
        x = x_ref[...]
    sh1 = mod_ref[:, 0:D]
    sc1 = mod_ref[:, D:2 * D]
    ms = jnp.mean(x * x, axis=-1, keepdims=True)
    h = (x * lax.rsqrt(ms + EPS)) * g_ref[...] * (1.0 + sc1) + sh1
    z = jnp.dot(h.astype(BF16), w_ref[...], preferred_element_type=F32) + b_ref[...]
    o = 0
    zf_ref[...] = z[:, o:o + FW].astype(BF16)
    o += FW
    za = z[:, o:o + CC]
    zb = z[:, o + CC:o + 2 * CC]
    v_ref[...] = (za * _sigmoid(zb)).astype(BF16)
    o += 2 * CC
    _tokens_to_chunks(z[:, o:o + SW], slab_ref, u_ref)
    o += SW
    gate_ref[...] = _sigmoid(z[:, o:]).astype(BF16)


def _inproj(layer, x, moe, mod_prev, mod, norm_g, w_in, b_in):
    has_moe = moe is not None
    row = lambda i: (i, 0)
    mod_spec = pl.BlockSpec((None, 1, NMOD * D), lambda i: (_mod_row(i), 0, 0))
    in_specs = [pl.BlockSpec((TB, D), row)]
    args = [x]
    if has_moe:
        in_specs += [pl.BlockSpec((TB * DC, LANES), row), mod_spec]
        args += [moe, mod_prev]
    in_specs += [mod_spec, _const_spec((1, D), layer), _const_spec((D, IN_COLS), layer),
                 _const_spec((1, IN_COLS), layer)]
    args += [mod, norm_g, w_in, b_in]
    out_specs = [pl.BlockSpec((TB, FW), row), pl.BlockSpec((TB, CC), row),
                 pl.BlockSpec((SG, CPB, CH * SH), lambda i: (0, i, 0)), pl.BlockSpec((TB, 3 * D), row)]
    out_shape = [jax.ShapeDtypeStruct((N, FW), BF16), jax.ShapeDtypeStruct((N, CC), BF16),
                 jax.ShapeDtypeStruct((SG, NCH, CH * SH), BF16), jax.ShapeDtypeStruct((N, 3 * D), BF16)]
    if has_moe:
        out_specs = [pl.BlockSpec((TB, D), row)] + out_specs
        out_shape = [jax.ShapeDtypeStruct((N, D), F32)] + out_shape
    outs = pl.pallas_call(
        functools.partial(_inproj_kernel, has_moe),
        grid=(NB,), in_specs=in_specs, out_specs=out_specs, out_shape=out_shape,
        scratch_shapes=[pltpu.VMEM((NM, TB, LANES), F32)],
        compiler_params=_cparams(), name=f"inproj_{int(has_moe)}",
    )(*args)
    if has_moe:
        return outs
    return [x] + list(outs)


def _fourier_kernel(scale, zf_ref, csbd_ref, csl_ref, wf_ref, o_ref):
    t = jnp.dot(zf_ref[...], csbd_ref[...], preferred_element_type=F32).astype(BF16)
    ts = jnp.concatenate([t[:, :FW], t[:, FW:]], axis=0)
    y = jnp.dot(csl_ref[...], ts, preferred_element_type=F32) * scale
    o_ref[...] = jnp.dot(y.astype(BF16), wf_ref[...], preferred_element_type=F32).astype(BF16)


def _dft_consts(L):
    k = np.arange(L)
    ang = 2.0 * np.pi * ((k[:, None] * k[None, :]) % L) / L
    csl = np.concatenate([np.cos(ang), -np.sin(ang)], axis=1)
    m = np.arange(FGD)
    angc = 2.0 * np.pi * ((m[:, None] * m[None, :]) % FGD) / FGD
    cbd = np.kron(np.eye(FG), np.cos(angc))
    sbd = np.kron(np.eye(FG), np.sin(angc))
    csbd = np.concatenate([cbd, sbd], axis=1)
    return jnp.asarray(csl, F32).astype(BF16), jnp.asarray(csbd, F32).astype(BF16)


def _fourier(layer, zf, w_four, L, first_block, nseq):
    csl, csbd = _dft_consts(L)
    scale = 1.0 / math.sqrt(L * FGD)
    return pl.pallas_call(
        functools.partial(_fourier_kernel, scale),
        grid=(nseq,),
        in_specs=[pl.BlockSpec((L, FW), lambda i: (i + first_block, 0)),
                  _const_spec((FW, 2 * FW)), _const_spec((L, 2 * L)), _const_spec((FW, D), layer)],
        out_specs=pl.BlockSpec((L, D), lambda i: (i, 0)),
        out_shape=jax.ShapeDtypeStruct((nseq * L, D), BF16),
        compiler_params=_cparams(), name=f"fourier_{L}",
    )(zf, csbd, csl, w_four)


HALO = 16
CONV_ROWS = 32


def _conv_kernel(vp_ref, vc_ref, vn_ref, cw_ref, cb_ref, lg_ref, lb_ref, wc_ref, o_ref, pad_ref, act_ref):
    i = pl.program_id(0)
    pos = (i - CTX_BLOCKS) % LAT_BLOCKS_PER_SEQ
    lat = i >= CTX_BLOCKS
    has_prev = jnp.logical_and(lat, pos != 0)
    has_next = jnp.logical_and(lat, pos != LAT_BLOCKS_PER_SEQ - 1)
    prev = vp_ref[TB - HALO:, :].astype(F32)
    nxt = vn_ref[:HALO, :].astype(F32)
    pad_ref[0:HALO, :] = jnp.where(has_prev, prev, 0.0)
    pad_ref[HALO:HALO + TB, :] = vc_ref[...].astype(F32)
    pad_ref[HALO + TB:, :] = jnp.where(has_next, nxt, 0.0)
    for r0 in range(0, TB, CONV_ROWS):
        acc = jnp.broadcast_to(cb_ref[...], (CONV_ROWS, CC))
        for s in range(SUB):
            part = None
            for q in range((TAPS + SUB) // SUB):
                o = SUB * q + s - (HALO - PAD)
                if 0 <= o < TAPS:
                    term = pad_ref[r0 + SUB * q:r0 + SUB * q + CONV_ROWS + SUB, :] * cw_ref[o:o + 1, :]
                    part = term if part is None else part + term
            acc = acc + part[s:s + CONV_ROWS, :]
        mu = jnp.mean(acc, axis=-1, keepdims=True)
        xc = acc - mu
        var = jnp.mean(xc * xc, axis=-1, keepdims=True)
        y = xc * lax.rsqrt(var + EPS) * lg_ref[...] + lb_ref[...]
        act_ref[r0:r0 + CONV_ROWS, :] = (y * _sigmoid(y)).astype(BF16)
    o_ref[...] = jnp.dot(act_ref[...], wc_ref[...], preferred_element_type=F32).astype(BF16)


def _conv(layer, v, conv_dw, conv_b, ln_g, ln_b, w_conv_out):
    return pl.pallas_call(
        _conv_kernel,
        grid=(NB,),
        in_specs=[pl.BlockSpec((TB, CC), lambda i: (jnp.maximum(i - 1, 0), 0)),
                  pl.BlockSpec((TB, CC), lambda i: (i, 0)),
                  pl.BlockSpec((TB, CC), lambda i: (jnp.minimum(i + 1, NB - 1), 0)),
                  _const_spec((TAPS, CC), layer), _const_spec((1, CC), layer), _const_spec((1, CC), layer),
                  _const_spec((1, CC), layer), _const_spec((CC, D), layer)],
        out_specs=pl.BlockSpec((TB, D), lambda i: (i, 0)),
        out_shape=jax.ShapeDtypeStruct((N, D), BF16),
        scratch_shapes=[pltpu.VMEM((TB + 2 * HALO, CC), F32), pltpu.VMEM((TB, CC), BF16)],
        compiler_params=_cparams(), name="conv_branch",
    )(v, v, v, conv_dw, conv_b, ln_g, ln_b, w_conv_out)


def _ssm_params(lam_re, lam_im, log_dt, b_re, b_im, c_re, c_im, d):
    dt = jnp.exp(log_dt)[..., None]
    ldr, ldi = lam_re * dt, lam_im * dt
    j = jnp.arange(CH + 1, dtype=F32)
    mag = jnp.exp(ldr[..., None] * j)
    pr, pi = mag * jnp.cos(ldi[..., None] * j), mag * jnp.sin(ldi[..., None] * j)
    nr, ni = pr[..., 1] - 1.0, pi[..., 1]
    den = lam_re * lam_re + lam_im * lam_im
    qr, qi = (nr * lam_re + ni * lam_im) / den, (ni * lam_re - nr * lam_im) / den
    bbr = qr[..., None] * b_re - qi[..., None] * b_im
    bbi = qr[..., None] * b_im + qi[..., None] * b_re
    xr = pr[..., :CH, None] * bbr[..., None, :] - pi[..., :CH, None] * bbi[..., None, :]
    xi = pr[..., :CH, None] * bbi[..., None, :] + pi[..., :CH, None] * bbr[..., None, :]
    kk = (jnp.einsum("dghp,dgpji->dgjhi", c_re, xr, precision=HI)
          - jnp.einsum("dghp,dgpji->dgjhi", c_im, xi, precision=HI))
    lag = np.arange(CH)[:, None, None]
    ki = np.arange(CH)[None, :, None]
    ti = np.arange(CH)[None, None, :]
    self_f = (ti - ki == lag).astype(np.float32)
    self_b = (ki - ti == lag).astype(np.float32)
    mm = (jnp.einsum("gjhi,jkt->gkith", kk[0], self_f, precision=HI)
          + jnp.einsum("gjhi,jkt->gkith", kk[1], self_b, precision=HI))
    eye_t = np.eye(CH, dtype=np.float32)
    eye_h = np.eye(SH, dtype=np.float32)
    dd = d.reshape(SG, SH)
    mm = mm + dd[:, None, :, None, None] * eye_t[None, :, None, :, None] * eye_h[None, None, :, None, :]
    m = mm.reshape(SG, CH * SH, CH * SH)
    wfr = jnp.transpose(xr[0][:, :, ::-1, :], (0, 2, 3, 1)).reshape(SG, CH * SH, SP)
    wfi = jnp.transpose(xi[0][:, :, ::-1, :], (0, 2, 3, 1)).reshape(SG, CH * SH, SP)
    wbr = jnp.transpose(xr[1], (0, 2, 3, 1)).reshape(SG, CH * SH, SP)
    wbi = jnp.transpose(xi[1], (0, 2, 3, 1)).reshape(SG, CH * SH, SP)
    ctr, cti = jnp.transpose(c_re, (0, 1, 3, 2)), jnp.transpose(c_im, (0, 1, 3, 2))
    pfr, pfi = pr[0][..., 1:], pi[0][..., 1:]
    pbr, pbi = pr[1][..., 1:][..., ::-1], pi[1][..., 1:][..., ::-1]
    afr = ctr[0][:, :, None, :] * pfr[..., None] - cti[0][:, :, None, :] * pfi[..., None]
    afi = ctr[0][:, :, None, :] * pfi[..., None] + cti[0][:, :, None, :] * pfr[..., None]
    abr = ctr[1][:, :, None, :] * pbr[..., None] - cti[1][:, :, None, :] * pbi[..., None]
    abi = ctr[1][:, :, None, :] * pbi[..., None] + cti[1][:, :, None, :] * pbr[..., None]
    ws = jnp.concatenate([w.astype(BF16) for w in (wfr, wfi, wbr, wbi)], axis=2)
    vs = jnp.concatenate([a.reshape(SG, SP, CH * SH).astype(BF16) for a in (afr, -afi, abr, -abi)], axis=1)
    a16 = jnp.stack([pr[0][..., CH], pi[0][..., CH], pr[1][..., CH], pi[1][..., CH]], 0)
    a16 = a16.reshape(4, NPAIR, 1, LANES)
    return m.astype(BF16), ws, vs, a16


def _ssm_kernel(u_ref, m_ref, ws_ref, vs_ref, a_ref, h0_ref, y_ref, fin_ref, s_ref, ef_ref):
    u0, u1 = u_ref[0], u_ref[1]
    s0 = jnp.dot(u0, ws_ref[0], preferred_element_type=F32)
    s1 = jnp.dot(u1, ws_ref[1], preferred_element_type=F32)
    for k in range(4):
        s_ref[k] = jnp.concatenate([s0[:, k * SP:(k + 1) * SP], s1[:, k * SP:(k + 1) * SP]], axis=1)

    def scan(kr, ki, first, nrows, nsteps, reverse, er, ei):
        ar, ai = a_ref[kr], a_ref[ki]
        order = range(nsteps - 1, -1, -1) if reverse else range(nsteps)
        for c in order:
            rows = pl.ds(first + c, nrows, stride=nsteps)
            ef_ref[kr, rows, :] = er
            ef_ref[ki, rows, :] = ei
            sr, si = s_ref[kr, rows, :], s_ref[ki, rows, :]
            er, ei = ar * er - ai * ei + sr, ar * ei + ai * er + si
        return er, ei

    zc = jnp.zeros((BATCH, LANES), F32)
    for kr, ki, rev in ((0, 1, False), (2, 3, True)):
        er, ei = scan(kr, ki, 0, BATCH, CPS_CTX, rev, zc, zc)
        fin_ref[kr] = er
        fin_ref[ki] = ei
        scan(kr, ki, NCH_CTX, DEC_BATCH, CPS_LAT, rev, h0_ref[kr], h0_ref[ki])

    for q, uq in ((0, u0), (1, u1)):
        ef = jnp.concatenate([ef_ref[k, :, q * SP:(q + 1) * SP] for k in range(4)], axis=1).astype(BF16)
        y_ref[q] = (jnp.dot(uq, m_ref[q], preferred_element_type=F32)
                    + jnp.dot(ef, vs_ref[q], preferred_element_type=F32))


def _ssm(layer, u_chunks, m, ws, vs, a16, h0):
    pair = lambda j: (j, 0, 0)
    return pl.pallas_call(
        _ssm_kernel,
        grid=(NPAIR,),
        in_specs=[pl.BlockSpec((2, NCH, CH * SH), pair),
                  pl.BlockSpec((None, 2, CH * SH, CH * SH), lambda j: (layer, j, 0, 0)),
                  pl.BlockSpec((None, 2, CH * SH, 4 * SP), lambda j: (layer, j, 0, 0)),
                  pl.BlockSpec((None, 2, 4 * SP, CH * SH), lambda j: (layer, j, 0, 0)),
                  pl.BlockSpec((None, 4, None, 1, LANES), lambda j: (layer, 0, j, 0, 0)),
                  pl.BlockSpec((None, 4, None, DEC_BATCH, LANES), lambda j: (layer, 0, j, 0, 0))],
        out_specs=[pl.BlockSpec((2, NCH, CH * SH), pair),
                   pl.BlockSpec((4, None, BATCH, LANES), lambda j: (0, j, 0, 0))],
        out_shape=[jax.ShapeDtypeStruct((SG, NCH, CH * SH), F32),
                   jax.ShapeDtypeStruct((4, NPAIR, BATCH, LANES), F32)],
        scratch_shapes=[pltpu.VMEM((4, NCH, LANES), F32), pltpu.VMEM((4, NCH, LANES), F32)],
        compiler_params=_cparams(), name="s5_chunked",
    )(u_chunks, m, ws, vs, a16, h0)


def _merge_kernel(x_ref, bfc_ref, bfl_ref, bc_ref, ys_ref, gate_ref, mod_ref, wg_ref, bg_ref, wso_ref, wo_ref,
                  n2_ref, rwh_ref, rwl_ref, rb_ref, xo_ref, h_ref, ti_ref, tp_ref, slab_ref):
    i = pl.program_id(0)
    g = jax.nn.gelu(_chunks_to_tokens(ys_ref, slab_ref))
    gl = jnp.dot(g.astype(BF16), wg_ref[...], preferred_element_type=F32) + bg_ref[...]
    y2 = g * _sigmoid(gl)
    brs = jnp.dot(y2.astype(BF16), wso_ref[...], preferred_element_type=F32)
    brf = jnp.where(i < CTX_BLOCKS, bfc_ref[...], bfl_ref[...]).astype(F32)
    mixed = (gate_ref[:, 0:D].astype(F32) * brf
             + gate_ref[:, D:2 * D].astype(F32) * bc_ref[...].astype(F32)
             + gate_ref[:, 2 * D:3 * D].astype(F32) * brs)
    g1 = mod_ref[:, 2 * D:3 * D]
    x = x_ref[...] + g1 * jnp.dot(mixed.astype(BF16), wo_ref[...], preferred_element_type=F32)
    xo_ref[...] = x
    sh2 = mod_ref[:, 3 * D:4 * D]
    sc2 = mod_ref[:, 4 * D:5 * D]
    ms = jnp.mean(x * x, axis=-1, keepdims=True)
    h = (x * lax.rsqrt(ms + EPS)) * n2_ref[...] * (1.0 + sc2) + sh2
    _std_to_rows(h_ref, h, TB)
    hh = h.astype(BF16)
    hl = (h - hh.astype(F32)).astype(BF16)
    logits = (jnp.dot(hh, rwh_ref[...], preferred_element_type=F32)
              + jnp.dot(hl, rwh_ref[...], preferred_element_type=F32)
              + jnp.dot(hh, rwl_ref[...], preferred_element_type=F32)) + rb_ref[...]
    lane = lax.broadcasted_iota(jnp.int32, (TB, LANES), 1).astype(F32)
    vals, idxs = [], []
    l = logits
    for _ in range(TOPK):
        m = jnp.max(l, axis=-1, keepdims=True)
        idx = jnp.min(jnp.where(l == m, lane, float(LANES)), axis=-1, keepdims=True)
        vals.append(m)
        idxs.append(idx)
        l = jnp.where(lane == idx, -jnp.inf, l)
    es = [jnp.exp(v - vals[0]) for v in vals]
    tot = es[0] + es[1] + es[2] + es[3]
    ti = jnp.zeros((TB, LANES), F32)
    tp = jnp.zeros((TB, LANES), F32)
    for k in range(TOPK):
        ti = jnp.where(lane == float(k), idxs[k], ti)
        tp = jnp.where(lane == float(k), es[k] / tot, tp)
    ti_ref[...] = ti.astype(jnp.int32)
    tp_ref[...] = tp


def _merge(layer, x, brf_ctx, brf_lat, brc, ys, gates, mod, w_glu, b_glu, w_ssm_out, w_out, norm2_g,
           rw_hi, rw_lo, rb):
    row = lambda i: (i, 0)
    return pl.pallas_call(
        _merge_kernel,
        grid=(NB,),
        in_specs=[pl.BlockSpec((TB, D), row),
                  pl.BlockSpec((TB, D), lambda i: (jnp.minimum(i, CTX_BLOCKS - 1), 0)),
                  pl.BlockSpec((TB, D), lambda i: (jnp.maximum(i - CTX_BLOCKS, 0), 0)),
                  pl.BlockSpec((TB, D), row),
                  pl.BlockSpec((SG, CPB, CH * SH), lambda i: (0, i, 0)),
                  pl.BlockSpec((TB, 3 * D), row),
                  pl.BlockSpec((None, 1, NMOD * D), lambda i: (_mod_row(i), 0, 0)),
                  _const_spec((SW, SW), layer), _const_spec((1, SW), layer), _const_spec((SW, D), layer),
                  _const_spec((D, D), layer), _const_spec((1, D), layer),
                  _const_spec((D, LANES), layer), _const_spec((D, LANES), layer), _const_spec((1, LANES), layer)],
        out_specs=[pl.BlockSpec((TB, D), row), pl.BlockSpec((TB * DC, LANES), row),
                   pl.BlockSpec((TB, LANES), row), pl.BlockSpec((TB, LANES), row)],
        out_shape=[jax.ShapeDtypeStruct((N, D), F32), jax.ShapeDtypeStruct((N * DC, LANES), F32),
                   jax.ShapeDtypeStruct((N, LANES), jnp.int32), jax.ShapeDtypeStruct((N, LANES), F32)],
        scratch_shapes=[pltpu.VMEM((NM, TB, LANES), F32)],
        compiler_params=_cparams(), name="merge_router",
    )(x, brf_ctx, brf_lat, brc, ys, gates, mod, w_glu, b_glu, w_ssm_out, w_out, norm2_g, rw_hi, rw_lo, rb)


def _routing_tables(top_i, top_p):
    e_flat = top_i[:, :TOPK].reshape(-1)
    p_flat = top_p[:, :TOPK].reshape(-1)
    na = N * TOPK
    key = e_flat * 32768 + jnp.arange(na, dtype=jnp.int32)
    skey, sw = lax.sort_key_val(key, p_flat)
    tok8 = ((skey & 32767) // TOPK) * DC
    tok8 = jnp.concatenate([tok8, jnp.zeros((TM,), jnp.int32)])
    sw = jnp.concatenate([sw, jnp.zeros((TM,), F32)])
    experts = jnp.arange(NE, dtype=jnp.int32)
    counts = jnp.sum((e_flat[:, None] == experts[None, :]).astype(jnp.int32), axis=0)
    ntile = (counts + TM - 1) // TM
    tend = jnp.cumsum(ntile)
    tstart = tend - ntile
    cstart = jnp.cumsum(counts) - counts
    n_tiles = tend[-1]
    later = jnp.logical_and(experts[None, :] > experts[:, None], ntile[None, :] > 0)
    nxt_e = jnp.min(jnp.where(later, experts[None, :], NE), axis=1)
    nxt_e = jnp.where(nxt_e == NE, -1, nxt_e)
    used = (ntile > 0).astype(jnp.int32)
    used_before = jnp.cumsum(used) - used
    ti = jnp.arange(NT_MAX + 1, dtype=jnp.int32)
    t = jnp.minimum(ti, n_tiles - 1)
    tile_e = jnp.minimum(jnp.sum((t[:, None] >= tend[None, :]).astype(jnp.int32), axis=1), NE - 1)
    onehot = (tile_e[:, None] == experts[None, :]).astype(jnp.int32)
    pick = lambda tab: jnp.sum(onehot * tab[None, :], axis=1)
    k = t - pick(tstart)
    tile_j0 = pick(cstart) + k * TM
    tile_cnt = jnp.clip(pick(counts) - k * TM, 0, TM)
    tile_first = jnp.logical_and(k == 0, ti < n_tiles)
    tile_slot = pick(used_before) % 2
    tile_next = pick(nxt_e)
    i32 = lambda a: a.astype(jnp.int32)
    return (tok8, sw, i32(tile_e), i32(tile_j0), i32(tile_cnt), i32(tile_first), i32(tile_slot), i32(tile_next),
            i32(n_tiles.reshape(1)))


GROWS = 16


def _gather_kernel(tok8_ref, j0_ref, nt_ref, h_ref, o_ref, g_ref):
    i = pl.program_id(0)

    @pl.when(i < nt_ref[0])
    def _():
        j0 = j0_ref[i]

        def body(b, carry):
            base = j0 + b * GROWS
            rows = [h_ref[pl.ds(pl.multiple_of(tok8_ref[base + j], DC), DC), :] for j in range(GROWS)]
            for j in range(GROWS):
                g_ref[pl.ds(pl.multiple_of((b * GROWS + j) * DC, DC), DC), :] = rows[j]
            return carry

        lax.fori_loop(0, TM // GROWS, body, 0)
        for c in range(DC):
            o_ref[:, c * LANES:(c + 1) * LANES] = g_ref[pl.ds(c, TM, stride=DC), :].astype(BF16)

    @pl.when(i >= nt_ref[0])
    def _():
        o_ref[...] = jnp.zeros_like(o_ref)


def _gather(h_rows, tok8, tile_j0, n_tiles):
    return pl.pallas_call(
        _gather_kernel,
        grid_spec=pltpu.PrefetchScalarGridSpec(
            num_scalar_prefetch=3, grid=(NT_MAX,),
            in_specs=[pl.BlockSpec((N * DC, LANES), lambda i, *_: (0, 0))],
            out_specs=pl.BlockSpec((TM, D), lambda i, *_: (i, 0)),
            scratch_shapes=[pltpu.VMEM((TM * DC, LANES), F32)]),
        out_shape=jax.ShapeDtypeStruct((R_MAX, D), BF16),
        compiler_params=_cparams(), name="moe_gather",
    )(tok8, tile_j0, n_tiles, h_rows)


def _experts_kernel(layer, tok8_ref, w_ref, te_ref, j0_ref, cnt_ref, first_ref, slot_ref, next_ref, nt_ref,
                    x_ref, wgu_hbm, bgu_ref, wdn_hbm, bdn_ref, acc_ref, y_ref, wgu_buf, wdn_buf, sem):
    i = pl.program_id(0)

    def weight_copies(e, slot):
        return (pltpu.make_async_copy(wgu_hbm.at[layer, e], wgu_buf.at[slot], sem.at[0, slot]),
                pltpu.make_async_copy(wdn_hbm.at[layer, e], wdn_buf.at[slot], sem.at[1, slot]))

    @pl.when(i == 0)
    def _():
        for cp in weight_copies(te_ref[0], 0):
            cp.start()
        acc_ref[...] = jnp.zeros_like(acc_ref)
        y_ref[...] = jnp.zeros_like(y_ref)

    live = i < nt_ref[0]
    slot = slot_ref[i]

    def scatter_previous_tile():
        p = jnp.maximum(i - 1, 0)
        j0 = j0_ref[p]
        cnt = jnp.where(i > 0, cnt_ref[p], 0)
        ys = y_ref.at[(i + 1) % 2]
        for r0 in range(0, TM, SCAT):
            offs, ws = [], []
            for j in range(SCAT):
                real = r0 + j < cnt
                offs.append(pl.multiple_of(jnp.where(real, tok8_ref[j0 + r0 + j], N * DC), DC))
                ws.append(jnp.where(real, w_ref[j0 + r0 + j], 0.0))
            olds = [acc_ref[pl.ds(o, DC), :] for o in offs]
            news = [olds[j] + ws[j] * ys[(r0 + j) * DC:(r0 + j + 1) * DC, :] for j in range(SCAT)]
            for j in range(SCAT):
                acc_ref[pl.ds(offs[j], DC), :] = news[j]

    @pl.when(jnp.logical_and(live, first_ref[i] == 1))
    def _():
        for cp in weight_copies(te_ref[i], slot):
            cp.wait()

        @pl.when(next_ref[i] >= 0)
        def _():
            for cp in weight_copies(next_ref[i], 1 - slot):
                cp.start()

    @pl.when(live)
    def _():
        scatter_previous_tile()
        x = x_ref[...]
        y = jnp.broadcast_to(bdn_ref[...], (TM, D))
        for f in range(0, DFF, FC):
            wg = wgu_buf[slot, :, f:f + FC].astype(BF16)
            wu = wgu_buf[slot, :, DFF + f:DFF + f + FC].astype(BF16)
            gate = jnp.dot(x, wg, preferred_element_type=F32) + bgu_ref[:, f:f + FC]
            up = jnp.dot(x, wu, preferred_element_type=F32) + bgu_ref[:, DFF + f:DFF + f + FC]
            gate = jnp.minimum(gate, LIMIT)
            up = jnp.clip(up, -LIMIT, LIMIT)
            act = (up + 1.0) * (gate * _sigmoid(ALPHA * gate))
            y = y + jnp.dot(act.astype(BF16), wdn_buf[slot, f:f + FC, :].astype(BF16), preferred_element_type=F32)
        _std_to_rows(y_ref.at[i % 2], y, TM)

    @pl.when(i == nt_ref[0])
    def _():
        scatter_previous_tile()


def _experts(layer, x_sorted, tables, w_gate_up, b_gate_up, w_down, b_down):
    tile = lambda i, *s: (jnp.minimum(i, s[-1][0] - 1), 0)
    bias = lambda i, *s: (layer, s[2][i], 0, 0)
    return pl.pallas_call(
        functools.partial(_experts_kernel, layer),
        grid_spec=pltpu.PrefetchScalarGridSpec(
            num_scalar_prefetch=len(tables), grid=(NT_MAX + 1,),
            in_specs=[pl.BlockSpec((TM, D), tile),
                      pl.BlockSpec(memory_space=pl.ANY),
                      pl.BlockSpec((None, None, 1, 2 * DFF), bias),
                      pl.BlockSpec(memory_space=pl.ANY),
                      pl.BlockSpec((None, None, 1, D), bias)],
            out_specs=pl.BlockSpec(((N + SUB) * DC, LANES), lambda i, *_: (0, 0)),
            scratch_shapes=[pltpu.VMEM((2, TM * DC, LANES), F32),
                            pltpu.VMEM((2, D, 2 * DFF), F32), pltpu.VMEM((2, DFF, D), F32),
                            pltpu.SemaphoreType.DMA((2, 2))]),
        out_shape=jax.ShapeDtypeStruct(((N + SUB) * DC, LANES), F32),
        compiler_params=_cparams(VMEM_MOE), name="moe_experts",
    )(*tables, x_sorted, w_gate_up, b_gate_up.reshape(DEPTH, NE, 1, 2 * DFF), w_down,
      b_down.reshape(DEPTH, NE, 1, D))


def _final_kernel(x_ref, moe_ref, mod_ref, g_ref, o_ref):
    x = x_ref[...] + mod_ref[:, 5 * D:6 * D] * _rows_to_std(moe_ref, TB)
    ms = jnp.mean(x * x, axis=-1, keepdims=True)
    o_ref[...] = (x * lax.rsqrt(ms + EPS)) * g_ref[...]


def _final(x, moe, mod, g):
    row = lambda i: (i, 0)
    return pl.pallas_call(
        _final_kernel,
        grid=(NB,),
        in_specs=[pl.BlockSpec((TB, D), row), pl.BlockSpec((TB * DC, LANES), row),
                  pl.BlockSpec((None, 1, NMOD * D), lambda i: (_mod_row(i), 0, 0)), _const_spec((1, D))],
        out_specs=pl.BlockSpec((TB, D), row),
        out_shape=jax.ShapeDtypeStruct((N, D), F32),
        compiler_params=_cparams(), name="final_norm",
    )(x, moe, mod, g)


def _grid_pos_embed(rows):
    quarter = D // 4
    freqs = jnp.exp(-math.log(10000.0) * jnp.arange(quarter, dtype=F32) / quarter)
    r = jnp.repeat(jnp.arange(rows, dtype=F32), GRID_W)
    col = jnp.tile(jnp.arange(GRID_W, dtype=F32), rows)
    ar = r[:, None] * freqs
    ac = col[:, None] * freqs
    return jnp.concatenate([jnp.sin(ar), jnp.cos(ar), jnp.sin(ac), jnp.cos(ac)], axis=-1)


def kernel(x_prompt, x_sample, state_ssm_re, state_ssm_im, c, c_ctx, w_mod, b_mod, norm1_g, norm2_g, w_in, b_in, w_four, conv_dw, conv_dw_b, conv_ln_g, conv_ln_b, w_conv_out, ssm_lam_re, ssm_lam_im, ssm_log_dt, ssm_b_re, ssm_b_im, ssm_c_re, ssm_c_im, ssm_d, w_ssm_glu, b_ssm_glu, w_ssm_out, w_out, router_w, router_b, w_gate_up, b_gate_up, w_down, b_down, final_norm_g):
    xs = x_sample + _grid_pos_embed(DEC_SEQ // GRID_W)[None]
    x = jnp.concatenate([x_prompt.reshape(N_CTX, D), xs.reshape(N_LAT, D)], axis=0)

    cond8 = jnp.concatenate([c_ctx[None, :], c, jnp.zeros((SUB - 1 - DEC_BATCH, D), F32)], axis=0)
    mod_all = _adaln(cond8, w_mod, b_mod)[:, :1 + DEC_BATCH].reshape(DEPTH, 1 + DEC_BATCH, 1, NMOD * D)

    w_in_b, w_four_b, w_conv_b = w_in.astype(BF16), w_four.astype(BF16), w_conv_out.astype(BF16)
    w_glu_b, w_sso_b, w_out_b = w_ssm_glu.astype(BF16), w_ssm_out.astype(BF16), w_out.astype(BF16)
    rw = jnp.pad(router_w, ((0, 0), (0, 0), (0, LANES - NE)))
    rw_hi = rw.astype(BF16)
    rw_lo = (rw - rw_hi.astype(F32)).astype(BF16)
    rb = jnp.pad(router_b, ((0, 0), (0, LANES - NE)), constant_values=-1e30).reshape(DEPTH, 1, LANES)
    r3 = lambda a: a.reshape(DEPTH, 1, a.shape[-1])

    m_all, ws_all, vs_all, a16_all = jax.vmap(_ssm_params)(
        ssm_lam_re, ssm_lam_im, ssm_log_dt, ssm_b_re, ssm_b_im, ssm_c_re, ssm_c_im, ssm_d)
    h0_all = jnp.stack([state_ssm_re[:, :, 0], state_ssm_im[:, :, 0], state_ssm_re[:, :, 1], state_ssm_im[:, :, 1]],
                       axis=0)
    h0_all = h0_all.reshape(4, DEC_BATCH, DEPTH, NPAIR, LANES).transpose(2, 0, 3, 1, 4)

    fins = []
    moe = None
    mod_prev = None
    for l in range(DEPTH):
        mod = mod_all[l]
        x, zf, v, u, gates = _inproj(l, x, moe, mod_prev, mod, r3(norm1_g), w_in_b, r3(b_in))
        brf_ctx = _fourier(l, zf, w_four_b, SEQ, 0, BATCH)
        brf_lat = _fourier(l, zf, w_four_b, DEC_SEQ, N_CTX // DEC_SEQ, DEC_BATCH)
        brc = _conv(l, v, conv_dw, r3(conv_dw_b), r3(conv_ln_g), r3(conv_ln_b), w_conv_b)
        ys, fin = _ssm(l, u, m_all, ws_all, vs_all, a16_all, h0_all)
        fins.append(fin)

        x, h_rows, top_i, top_p = _merge(l, x, brf_ctx, brf_lat, brc, ys, gates, mod, w_glu_b, r3(b_ssm_glu),
                                         w_sso_b, w_out_b, r3(norm2_g), rw_hi, rw_lo, rb)
        tables = _routing_tables(top_i, top_p)
        x_sorted = _gather(h_rows, tables[0], tables[3], tables[-1])
        moe = _experts(l, x_sorted, tables, w_gate_up, b_gate_up, w_down, b_down)
        mod_prev = mod

    y = _final(x, moe, mod_prev, final_norm_g.reshape(1, D))
    fin = jnp.stack(fins, axis=0).transpose(3, 0, 1, 2, 4).reshape(BATCH, DEPTH, 2, 2, SG, SP)
    new_re = fin[:, :, :, 0]
    new_im = fin[:, :, :, 1]
    return (y[:N_CTX].reshape(BATCH, SEQ, D), y[N_CTX:].reshape(DEC_BATCH, DEC_SEQ, D), new_re, new_im)
```

```python
import functools
import math

import numpy as np
import jax
import jax.numpy as jnp
from jax import lax
from jax.experimental import pallas as pl
from jax.experimental.pallas import tpu as pltpu

F32 = jnp.float32
BF16 = jnp.bfloat16
HI = lax.Precision.HIGHEST

D = 1024
BATCH, SEQ = 16, 256
DEC_BATCH, DEC_SEQ = 2, 1024
DEPTH = 4
N_CTX = BATCH * SEQ
N_LAT = DEC_BATCH * DEC_SEQ
N = N_CTX + N_LAT
GRID_W = 64
FW, FG = 512, 4
FGD = FW // FG
CC, TAPS = 512, 31
PAD = TAPS // 2
SW, SH = 512, 16
SG = SW // SH
SP = 64
IN_COLS = FW + 2 * CC + SW + 3 * D
NE, TOPK, DFF = 32, 4, 1024
LIMIT, ALPHA = 7.0, 1.702
EPS = 1e-6
NMOD = 6

TB = 256
NB = N // TB
CTX_BLOCKS = N_CTX // TB
LAT_BLOCKS_PER_SEQ = DEC_SEQ // TB
CH = 16
NCH = N // CH
NCH_CTX = N_CTX // CH
CPS_CTX = SEQ // CH
CPS_LAT = DEC_SEQ // CH
NPAIR = SG // 2
TM = 256
NT_MAX = (N * TOPK + NE * (TM - 1) + TM - 1) // TM
R_MAX = NT_MAX * TM
SUB = 8
LANES = 128
DC = D // LANES
SCAT = 16
FC = 1024
VMEM_DEFAULT = 48 * 1024 * 1024
VMEM_MOE = 62 * 1024 * 1024


def _cparams(vmem=VMEM_DEFAULT, ndim=1):
    return pltpu.CompilerParams(dimension_semantics=("arbitrary",) * ndim, vmem_limit_bytes=vmem)


def _const_spec(shape, layer=None):
    if layer is None:
        zeros = (0,) * len(shape)
        return pl.BlockSpec(shape, lambda *_: zeros)
    zeros = (0,) * len(shape)
    return pl.BlockSpec((None,) + tuple(shape), lambda *_: (layer,) + zeros, pipeline_mode=pl.Buffered(1))


def _mod_row(i):
    return jnp.where(i < CTX_BLOCKS, 0, 1 + (i - CTX_BLOCKS) // LAT_BLOCKS_PER_SEQ)


def _sigmoid(x):
    return 1.0 / (1.0 + jnp.exp(-x))


def _rows_to_std(ref, rows):
    return jnp.concatenate([ref[pl.ds(c, rows, stride=DC), :] for c in range(DC)], axis=1)


def _std_to_rows(ref, val, rows):
    for c in range(DC):
        ref[pl.ds(c, rows, stride=DC), :] = val[:, c * LANES:(c + 1) * LANES]


NM = SW // LANES
GPV = LANES // SH
CPB = TB // CH


def _lane_group():
    return lax.broadcasted_iota(jnp.int32, (CPB, LANES), 1) // SH


def _tokens_to_chunks(u, slab_ref, out_ref):
    for m in range(NM):
        slab_ref[m] = u[:, m * LANES:(m + 1) * LANES]
    grp = _lane_group()
    for m in range(NM):
        for kk in range(CH // GPV):
            zs = [slab_ref[m, pl.ds(kk * GPV + k8, CPB, stride=CH), :] for k8 in range(GPV)]
            for r in range(GPV):
                acc = None
                for k8 in range(GPV):
                    sh = ((k8 - r) % GPV) * SH
                    piece = zs[k8] if sh == 0 else pltpu.roll(zs[k8], sh, 1)
                    acc = piece if acc is None else jnp.where(grp == k8, piece, acc)
                out_ref[m * GPV + r, :, kk * LANES:(kk + 1) * LANES] = acc.astype(out_ref.dtype)


def _chunks_to_tokens(y_ref, slab_ref):
    grp = _lane_group()
    for m in range(NM):
        for kk in range(CH // GPV):
            ys = [y_ref[m * GPV + r, :, kk * LANES:(kk + 1) * LANES] for r in range(GPV)]
            for t8 in range(GPV):
                acc = None
                for r in range(GPV):
                    sh = ((r - t8) % GPV) * SH
                    piece = ys[r] if sh == 0 else pltpu.roll(ys[r], sh, 1)
                    acc = piece if acc is None else jnp.where(grp == r, piece, acc)
                slab_ref[m, pl.ds(kk * GPV + t8, CPB, stride=CH), :] = acc
    return jnp.concatenate([slab_ref[m] for m in range(NM)], axis=1)


MOD_COLS = 1536


def _mod_kernel(cond_ref, w_ref, b_ref, o_ref):
    c = cond_ref[...]
    s = c * _sigmoid(c)
    o_ref[...] = jnp.dot(s.astype(BF16), w_ref[...].astype(BF16), preferred_element_type=F32) + b_ref[...]


def _adaln(cond8, w_mod, b_mod):
    nc = NMOD * D // MOD_COLS
    return pl.pallas_call(
        _mod_kernel,
        grid=(DEPTH, nc),
        in_specs=[
            pl.BlockSpec((SUB, D), lambda l, j: (0, 0)),
            pl.BlockSpec((None, D, MOD_COLS), lambda l, j: (l, 0, j)),
            pl.BlockSpec((None, 1, MOD_COLS), lambda l, j: (l, 0, j)),
        ],
        out_specs=pl.BlockSpec((None, SUB, MOD_COLS), lambda l, j: (l, 0, j)),
        out_shape=jax.ShapeDtypeStruct((DEPTH, SUB, NMOD * D), F32),
        compiler_params=_cparams(ndim=2),
        name="adaln_mod",
    )(cond8, w_mod, b_mod.reshape(DEPTH, 1, NMOD * D))


def _inproj_kernel(has_moe, *refs):
    if has_moe:
        (x_ref, moe_ref, modp_ref, mod_ref, g_ref, w_ref, b_ref,
         xo_ref, zf_ref, v_ref, u_ref, gate_ref, slab_ref) = refs
        x = x_ref[...] + modp_ref[:, 5 * D:6 * D] * _rows_to_std(moe_ref, TB)
        xo_ref[...] = x
    else:
        x_ref, mod_ref, g_ref, w_ref, b_ref, zf_ref, v_ref, u_ref, gate_ref, slab_ref = refs
        x = x_ref[...]
    sh1 = mod_ref[:, 0:D]
    sc1 = mod_ref[:, D:2 * D]
    ms = jnp.mean(x * x, axis=-1, keepdims=True)
    h = (x * lax.rsqrt(ms + EPS)) * g_ref[...] * (1.0 + sc1) + sh1
    z = jnp.dot(h.astype(BF16), w_ref[...], preferred_element_type=F32) + b_ref[...]
    o = 0
    zf_ref[...] = z[:, o:o + FW].astype(BF16)
    o += FW
    za = z[:, o:o + CC]
    zb = z[:, o + CC:o + 2 * CC]
    v_ref[...] = (za * _sigmoid(zb)).astype(BF16)
    o += 2 * CC
    _tokens_to_chunks(z[:, o:o + SW], slab_ref, u_ref)
    o += SW
    gate_ref[...] = _sigmoid(z[:, o:]).astype(BF16)


def _inproj(layer, x, moe, mod_prev, mod, norm_g, w_in, b_in):
    has_moe = moe is not None
    row = lambda i: (i, 0)
    mod_spec = pl.BlockSpec((None, 1, NMOD * D), lambda i: (_mod_row(i), 0, 0))
    in_specs = [pl.BlockSpec((TB, D), row)]
    args = [x]
    if has_moe:
        in_specs += [pl.BlockSpec((TB * DC, LANES), row), mod_spec]
        args += [moe, mod_prev]
    in_specs += [mod_spec, _const_spec((1, D), layer), _const_spec((D, IN_COLS), layer),
                 _const_spec((1, IN_COLS), layer)]
    args += [mod, norm_g, w_in, b_in]
    out_specs = [pl.BlockSpec((TB, FW), row), pl.BlockSpec((TB, CC), row),
                 pl.BlockSpec((SG, CPB, CH * SH), lambda i: (0, i, 0)), pl.BlockSpec((TB, 3 * D), row)]
    out_shape = [jax.ShapeDtypeStruct((N, FW), BF16), jax.ShapeDtypeStruct((N, CC), BF16),
                 jax.ShapeDtypeStruct((SG, NCH, CH * SH), BF16), jax.ShapeDtypeStruct((N, 3 * D), BF16)]
    if has_moe:
        out_specs = [pl.BlockSpec((TB, D), row)] + out_specs
        out_shape = [jax.ShapeDtypeStruct((N, D), F32)] + out_shape
    outs = pl.pallas_call(
        functools.partial(_inproj_kernel, has_moe),
        grid=(NB,), in_specs=in_specs, out_specs=out_specs, out_shape=out_shape,
        scratch_shapes=[pltpu.VMEM((NM, TB, LANES), F32)],
        compiler_params=_cparams(), name=f"inproj_{int(has_moe)}",
    )(*args)
    if has_moe:
        return outs
    return [x] + list(outs)


def _fourier_kernel(scale, zf_ref, csbd_ref, csl_ref, wf_ref, o_ref):
    t = jnp.dot(zf_ref[...], csbd_ref[...], preferred_element_type=F32).astype(BF16)
    ts = jnp.concatenate([t[:, :FW], t[:, FW:]], axis=0)
    y = jnp.dot(csl_ref[...], ts, preferred_element_type=F32) * scale
    o_ref[...] = jnp.dot(y.astype(BF16), wf_ref[...], preferred_element_type=F32).astype(BF16)


def _dft_consts(L):
    k = np.arange(L)
    ang = 2.0 * np.pi * ((k[:, None] * k[None, :]) % L) / L
    csl = np.concatenate([np.cos(ang), -np.sin(ang)], axis=1)
    m = np.arange(FGD)
    angc = 2.0 * np.pi * ((m[:, None] * m[None, :]) % FGD) / FGD
    cbd = np.kron(np.eye(FG), np.cos(angc))
    sbd = np.kron(np.eye(FG), np.sin(angc))
    csbd = np.concatenate([cbd, sbd], axis=1)
    return jnp.asarray(csl, F32).astype(BF16), jnp.asarray(csbd, F32).astype(BF16)


def _fourier(layer, zf, w_four, L, first_block, nseq):
    csl, csbd = _dft_consts(L)
    scale = 1.0 / math.sqrt(L * FGD)
    return pl.pallas_call(
        functools.partial(_fourier_kernel, scale),
        grid=(nseq,),
        in_specs=[pl.BlockSpec((L, FW), lambda i: (i + first_block, 0)),
                  _const_spec((FW, 2 * FW)), _const_spec((L, 2 * L)), _const_spec((FW, D), layer)],
        out_specs=pl.BlockSpec((L, D), lambda i: (i, 0)),
        out_shape=jax.ShapeDtypeStruct((nseq * L, D), BF16),
        compiler_params=_cparams(), name=f"fourier_{L}",
    )(zf, csbd, csl, w_four)


HALO = 16
CONV_ROWS = 32


def _conv_kernel(vp_ref, vc_ref, vn_ref, cw_ref, cb_ref, lg_ref, lb_ref, wc_ref, o_ref, pad_ref, act_ref):
    i = pl.program_id(0)
    pos = (i - CTX_BLOCKS) % LAT_BLOCKS_PER_SEQ
    lat = i >= CTX_BLOCKS
    has_prev = jnp.logical_and(lat, pos != 0)
    has_next = jnp.logical_and(lat, pos != LAT_BLOCKS_PER_SEQ - 1)
    prev = vp_ref[TB - HALO:, :].astype(F32)
    nxt = vn_ref[:HALO, :].astype(F32)
    pad_ref[0:HALO, :] = jnp.where(has_prev, prev, 0.0)
    pad_ref[HALO:HALO + TB, :] = vc_ref[...].astype(F32)
    pad_ref[HALO + TB:, :] = jnp.where(has_next, nxt, 0.0)
    for r0 in range(0, TB, CONV_ROWS):
        acc = jnp.broadcast_to(cb_ref[...], (CONV_ROWS, CC))
        for s in range(SUB):
            part = None
            for q in range((TAPS + SUB) // SUB):
                o = SUB * q + s - (HALO - PAD)
                if 0 <= o < TAPS:
                    term = pad_ref[r0 + SUB * q:r0 + SUB * q + CONV_ROWS + SUB, :] * cw_ref[o:o + 1, :]
                    part = term if part is None else part + term
            acc = acc + part[s:s + CONV_ROWS, :]
        mu = jnp.mean(acc, axis=-1, keepdims=True)
        xc = acc - mu
        var = jnp.mean(xc * xc, axis=-1, keepdims=True)
        y = xc * lax.rsqrt(var + EPS) * lg_ref[...] + lb_ref[...]
        act_ref[r0:r0 + CONV_ROWS, :] = (y * _sigmoid(y)).astype(BF16)
    o_ref[...] = jnp.dot(act_ref[...], wc_ref[...], preferred_element_type=F32).astype(BF16)


def _conv(layer, v, conv_dw, conv_b, ln_g, ln_b, w_conv_out):
    return pl.pallas_call(
        _conv_kernel,
        grid=(NB,),
        in_specs=[pl.BlockSpec((TB, CC), lambda i: (jnp.maximum(i - 1, 0), 0)),
                  pl.BlockSpec((TB, CC), lambda i: (i, 0)),
                  pl.BlockSpec((TB, CC), lambda i: (jnp.minimum(i + 1, NB - 1), 0)),
                  _const_spec((TAPS, CC), layer), _const_spec((1, CC), layer), _const_spec((1, CC), layer),
                  _const_spec((1, CC), layer), _const_spec((CC, D), layer)],
        out_specs=pl.BlockSpec((TB, D), lambda i: (i, 0)),
        out_shape=jax.ShapeDtypeStruct((N, D), BF16),
        scratch_shapes=[pltpu.VMEM((TB + 2 * HALO, CC), F32), pltpu.VMEM((TB, CC), BF16)],
        compiler_params=_cparams(), name="conv_branch",
    )(v, v, v, conv_dw, conv_b, ln_g, ln_b, w_conv_out)


def _ssm_params(lam_re, lam_im, log_dt, b_re, b_im, c_re, c_im, d):
    dt = jnp.exp(log_dt)[..., None]
    ldr, ldi = lam_re * dt, lam_im * dt
    j = jnp.arange(CH + 1, dtype=F32)
    mag = jnp.exp(ldr[..., None] * j)
    pr, pi = mag * jnp.cos(ldi[..., None] * j), mag * jnp.sin(ldi[..., None] * j)
    nr, ni = pr[..., 1] - 1.0, pi[..., 1]
    den = lam_re * lam_re + lam_im * lam_im
    qr, qi = (nr * lam_re + ni * lam_im) / den, (ni * lam_re - nr * lam_im) / den
    bbr = qr[..., None] * b_re - qi[..., None] * b_im
    bbi = qr[..., None] * b_im + qi[..., None] * b_re
    xr = pr[..., :CH, None] * bbr[..., None, :] - pi[..., :CH, None] * bbi[..., None, :]
    xi = pr[..., :CH, None] * bbi[..., None, :] + pi[..., :CH, None] * bbr[..., None, :]
    kk = (jnp.einsum("dghp,dgpji->dgjhi", c_re, xr, precision=HI)
          - jnp.einsum("dghp,dgpji->dgjhi", c_im, xi, precision=HI))
    kf = jnp.transpose(kk[0], (0, 3, 1, 2))
    kb = jnp.transpose(kk[1], (0, 3, 1, 2))
    eye_h = np.eye(SH, dtype=np.float32)
    centre = kf[:, :, 0:1, :] + kb[:, :, 0:1, :] + d.reshape(SG, 1, 1, SH) * eye_h[None, :, None, :]
    lags = jnp.concatenate([kb[:, :, :0:-1, :], centre, kf[:, :, 1:, :], jnp.zeros((SG, SH, 1, SH), F32)], axis=2)
    toe = jnp.tile(lags, (1, 1, CH, 1))[:, :, :CH * (2 * CH - 1), :].reshape(SG, SH, CH, 2 * CH - 1, SH)
    toe = toe[:, :, :, CH - 1:, :]
    m = jnp.transpose(toe, (0, 2, 1, 3, 4)).reshape(SG, CH * SH, CH * SH)
    wfr = jnp.transpose(xr[0][:, :, ::-1, :], (0, 2, 3, 1)).reshape(SG, CH * SH, SP)
    wfi = jnp.transpose(xi[0][:, :, ::-1, :], (0, 2, 3, 1)).reshape(SG, CH * SH, SP)
    wbr = jnp.transpose(xr[1], (0, 2, 3, 1)).reshape(SG, CH * SH, SP)
    wbi = jnp.transpose(xi[1], (0, 2, 3, 1)).reshape(SG, CH * SH, SP)
    ctr, cti = jnp.transpose(c_re, (0, 1, 3, 2)), jnp.transpose(c_im, (0, 1, 3, 2))
    pfr, pfi = pr[0][..., 1:], pi[0][..., 1:]
    pbr, pbi = pr[1][..., 1:][..., ::-1], pi[1][..., 1:][..., ::-1]
    afr = ctr[0][:, :, None, :] * pfr[..., None] - cti[0][:, :, None, :] * pfi[..., None]
    afi = ctr[0][:, :, None, :] * pfi[..., None] + cti[0][:, :, None, :] * pfr[..., None]
    abr = ctr[1][:, :, None, :] * pbr[..., None] - cti[1][:, :, None, :] * pbi[..., None]
    abi = ctr[1][:, :, None, :] * pbi[..., None] + cti[1][:, :, None, :] * pbr[..., None]
    ws = jnp.concatenate([w.astype(BF16) for w in (wfr, wfi, wbr, wbi)], axis=2)
    vs = jnp.concatenate([a.reshape(SG, SP, CH * SH).astype(BF16) for a in (afr, -afi, abr, -abi)], axis=1)
    a16 = jnp.stack([pr[0][..., CH], pi[0][..., CH], pr[1][..., CH], pi[1][..., CH]], 0)
    a16 = a16.reshape(4, NPAIR, 1, LANES)
    return m.astype(BF16), ws, vs, a16


def _ssm_kernel(u_ref, m_ref, ws_ref, vs_ref, a_ref, h0_ref, y_ref, fin_ref, s_ref, ef_ref):
    u0, u1 = u_ref[0], u_ref[1]
    s0 = jnp.dot(u0, ws_ref[0], preferred_element_type=F32)
    s1 = jnp.dot(u1, ws_ref[1], preferred_element_type=F32)
    for k in range(4):
        s_ref[k] = jnp.concatenate([s0[:, k * SP:(k + 1) * SP], s1[:, k * SP:(k + 1) * SP]], axis=1)

    def scan(kr, ki, first, nrows, nsteps, reverse, er, ei):
        ar, ai = a_ref[kr], a_ref[ki]
        order = range(nsteps - 1, -1, -1) if reverse else range(nsteps)
        for c in order:
            rows = pl.ds(first + c, nrows, stride=nsteps)
            ef_ref[kr, rows, :] = er
            ef_ref[ki, rows, :] = ei
            sr, si = s_ref[kr, rows, :], s_ref[ki, rows, :]
            er, ei = ar * er - ai * ei + sr, ar * ei + ai * er + si
        return er, ei

    zc = jnp.zeros((BATCH, LANES), F32)
    for kr, ki, rev in ((0, 1, False), (2, 3, True)):
        er, ei = scan(kr, ki, 0, BATCH, CPS_CTX, rev, zc, zc)
        fin_ref[kr] = er
        fin_ref[ki] = ei
        scan(kr, ki, NCH_CTX, DEC_BATCH, CPS_LAT, rev, h0_ref[kr], h0_ref[ki])

    for q, uq in ((0, u0), (1, u1)):
        ef = jnp.concatenate([ef_ref[k, :, q * SP:(q + 1) * SP] for k in range(4)], axis=1).astype(BF16)
        y_ref[q] = (jnp.dot(uq, m_ref[q], preferred_element_type=F32)
                    + jnp.dot(ef, vs_ref[q], preferred_element_type=F32))


def _ssm(layer, u_chunks, m, ws, vs, a16, h0):
    pair = lambda j: (j, 0, 0)
    return pl.pallas_call(
        _ssm_kernel,
        grid=(NPAIR,),
        in_specs=[pl.BlockSpec((2, NCH, CH * SH), pair),
                  pl.BlockSpec((None, 2, CH * SH, CH * SH), lambda j: (layer, j, 0, 0)),
                  pl.BlockSpec((None, 2, CH * SH, 4 * SP), lambda j: (layer, j, 0, 0)),
                  pl.BlockSpec((None, 2, 4 * SP, CH * SH), lambda j: (layer, j, 0, 0)),
                  pl.BlockSpec((None, 4, None, 1, LANES), lambda j: (layer, 0, j, 0, 0)),
                  pl.BlockSpec((None, 4, None, DEC_BATCH, LANES), lambda j: (layer, 0, j, 0, 0))],
        out_specs=[pl.BlockSpec((2, NCH, CH * SH), pair),
                   pl.BlockSpec((4, None, BATCH, LANES), lambda j: (0, j, 0, 0))],
        out_shape=[jax.ShapeDtypeStruct((SG, NCH, CH * SH), F32),
                   jax.ShapeDtypeStruct((4, NPAIR, BATCH, LANES), F32)],
        scratch_shapes=[pltpu.VMEM((4, NCH, LANES), F32), pltpu.VMEM((4, NCH, LANES), F32)],
        compiler_params=_cparams(), name="s5_chunked",
    )(u_chunks, m, ws, vs, a16, h0)


def _merge_kernel(x_ref, bfc_ref, bfl_ref, bc_ref, ys_ref, gate_ref, mod_ref, wg_ref, bg_ref, wso_ref, wo_ref,
                  n2_ref, rwh_ref, rwl_ref, rb_ref, xo_ref, h_ref, ti_ref, tp_ref, slab_ref):
    i = pl.program_id(0)
    g = jax.nn.gelu(_chunks_to_tokens(ys_ref, slab_ref))
    gl = jnp.dot(g.astype(BF16), wg_ref[...], preferred_element_type=F32) + bg_ref[...]
    y2 = g * _sigmoid(gl)
    brs = jnp.dot(y2.astype(BF16), wso_ref[...], preferred_element_type=F32)
    brf = jnp.where(i < CTX_BLOCKS, bfc_ref[...], bfl_ref[...]).astype(F32)
    mixed = (gate_ref[:, 0:D].astype(F32) * brf
             + gate_ref[:, D:2 * D].astype(F32) * bc_ref[...].astype(F32)
             + gate_ref[:, 2 * D:3 * D].astype(F32) * brs)
    g1 = mod_ref[:, 2 * D:3 * D]
    x = x_ref[...] + g1 * jnp.dot(mixed.astype(BF16), wo_ref[...], preferred_element_type=F32)
    xo_ref[...] = x
    sh2 = mod_ref[:, 3 * D:4 * D]
    sc2 = mod_ref[:, 4 * D:5 * D]
    ms = jnp.mean(x * x, axis=-1, keepdims=True)
    h = (x * lax.rsqrt(ms + EPS)) * n2_ref[...] * (1.0 + sc2) + sh2
    _std_to_rows(h_ref, h, TB)
    hh = h.astype(BF16)
    hl = (h - hh.astype(F32)).astype(BF16)
    logits = (jnp.dot(hh, rwh_ref[...], preferred_element_type=F32)
              + jnp.dot(hl, rwh_ref[...], preferred_element_type=F32)
              + jnp.dot(hh, rwl_ref[...], preferred_element_type=F32)) + rb_ref[...]
    lane = lax.broadcasted_iota(jnp.int32, (TB, LANES), 1).astype(F32)
    vals, idxs = [], []
    l = logits
    for _ in range(TOPK):
        m = jnp.max(l, axis=-1, keepdims=True)
        idx = jnp.min(jnp.where(l == m, lane, float(LANES)), axis=-1, keepdims=True)
        vals.append(m)
        idxs.append(idx)
        l = jnp.where(lane == idx, -jnp.inf, l)
    es = [jnp.exp(v - vals[0]) for v in vals]
    tot = es[0] + es[1] + es[2] + es[3]
    ti = jnp.zeros((TB, LANES), F32)
    tp = jnp.zeros((TB, LANES), F32)
    for k in range(TOPK):
        ti = jnp.where(lane == float(k), idxs[k], ti)
        tp = jnp.where(lane == float(k), es[k] / tot, tp)
    ti_ref[...] = ti.astype(jnp.int32)
    tp_ref[...] = tp


def _merge(layer, x, brf_ctx, brf_lat, brc, ys, gates, mod, w_glu, b_glu, w_ssm_out, w_out, norm2_g,
           rw_hi, rw_lo, rb):
    row = lambda i: (i, 0)
    return pl.pallas_call(
        _merge_kernel,
        grid=(NB,),
        in_specs=[pl.BlockSpec((TB, D), row),
                  pl.BlockSpec((TB, D), lambda i: (jnp.minimum(i, CTX_BLOCKS - 1), 0)),
                  pl.BlockSpec((TB, D), lambda i: (jnp.maximum(i - CTX_BLOCKS, 0), 0)),
                  pl.BlockSpec((TB, D), row),
                  pl.BlockSpec((SG, CPB, CH * SH), lambda i: (0, i, 0)),
                  pl.BlockSpec((TB, 3 * D), row),
                  pl.BlockSpec((None, 1, NMOD * D), lambda i: (_mod_row(i), 0, 0)),
                  _const_spec((SW, SW), layer), _const_spec((1, SW), layer), _const_spec((SW, D), layer),
                  _const_spec((D, D), layer), _const_spec((1, D), layer),
                  _const_spec((D, LANES), layer), _const_spec((D, LANES), layer), _const_spec((1, LANES), layer)],
        out_specs=[pl.BlockSpec((TB, D), row), pl.BlockSpec((TB * DC, LANES), row),
                   pl.BlockSpec((TB, LANES), row), pl.BlockSpec((TB, LANES), row)],
        out_shape=[jax.ShapeDtypeStruct((N, D), F32), jax.ShapeDtypeStruct((N * DC, LANES), F32),
                   jax.ShapeDtypeStruct((N, LANES), jnp.int32), jax.ShapeDtypeStruct((N, LANES), F32)],
        scratch_shapes=[pltpu.VMEM((NM, TB, LANES), F32)],
        compiler_params=_cparams(), name="merge_router",
    )(x, brf_ctx, brf_lat, brc, ys, gates, mod, w_glu, b_glu, w_ssm_out, w_out, norm2_g, rw_hi, rw_lo, rb)


def _routing_tables(top_i, top_p):
    e_flat = top_i[:, :TOPK].reshape(-1)
    p_flat = top_p[:, :TOPK].reshape(-1)
    na = N * TOPK
    key = e_flat * 32768 + jnp.arange(na, dtype=jnp.int32)
    skey, sw = lax.sort_key_val(key, p_flat)
    tok8 = ((skey & 32767) // TOPK) * DC
    tok8 = jnp.concatenate([tok8, jnp.zeros((TM,), jnp.int32)])
    sw = jnp.concatenate([sw, jnp.zeros((TM,), F32)])
    experts = jnp.arange(NE, dtype=jnp.int32)
    counts = jnp.sum((e_flat[:, None] == experts[None, :]).astype(jnp.int32), axis=0)
    ntile = (counts + TM - 1) // TM
    tend = jnp.cumsum(ntile)
    tstart = tend - ntile
    cstart = jnp.cumsum(counts) - counts
    n_tiles = tend[-1]
    later = jnp.logical_and(experts[None, :] > experts[:, None], ntile[None, :] > 0)
    nxt_e = jnp.min(jnp.where(later, experts[None, :], NE), axis=1)
    nxt_e = jnp.where(nxt_e == NE, -1, nxt_e)
    used = (ntile > 0).astype(jnp.int32)
    used_before = jnp.cumsum(used) - used
    ti = jnp.arange(NT_MAX + 1, dtype=jnp.int32)
    t = jnp.minimum(ti, n_tiles - 1)
    tile_e = jnp.minimum(jnp.sum((t[:, None] >= tend[None, :]).astype(jnp.int32), axis=1), NE - 1)
    onehot = (tile_e[:, None] == experts[None, :]).astype(jnp.int32)
    pick = lambda tab: jnp.sum(onehot * tab[None, :], axis=1)
    k = t - pick(tstart)
    tile_j0 = pick(cstart) + k * TM
    tile_cnt = jnp.clip(pick(counts) - k * TM, 0, TM)
    tile_first = jnp.logical_and(k == 0, ti < n_tiles)
    tile_slot = pick(used_before) % 2
    tile_next = pick(nxt_e)
    i32 = lambda a: a.astype(jnp.int32)
    return (tok8, sw, i32(tile_e), i32(tile_j0), i32(tile_cnt), i32(tile_first), i32(tile_slot), i32(tile_next),
            i32(n_tiles.reshape(1)))


GROWS = 16


def _gather_kernel(tok8_ref, j0_ref, nt_ref, h_ref, o_ref, g_ref):
    i = pl.program_id(0)

    @pl.when(i < nt_ref[0])
    def _():
        j0 = j0_ref[i]

        def body(b, carry):
            base = j0 + b * GROWS
            rows = [h_ref[pl.ds(pl.multiple_of(tok8_ref[base + j], DC), DC), :] for j in range(GROWS)]
            for j in range(GROWS):
                g_ref[pl.ds(pl.multiple_of((b * GROWS + j) * DC, DC), DC), :] = rows[j]
            return carry

        lax.fori_loop(0, TM // GROWS, body, 0)
        for c in range(DC):
            o_ref[:, c * LANES:(c + 1) * LANES] = g_ref[pl.ds(c, TM, stride=DC), :].astype(BF16)

    @pl.when(i >= nt_ref[0])
    def _():
        o_ref[...] = jnp.zeros_like(o_ref)


def _gather(h_rows, tok8, tile_j0, n_tiles):
    return pl.pallas_call(
        _gather_kernel,
        grid_spec=pltpu.PrefetchScalarGridSpec(
            num_scalar_prefetch=3, grid=(NT_MAX,),
            in_specs=[pl.BlockSpec((N * DC, LANES), lambda i, *_: (0, 0))],
            out_specs=pl.BlockSpec((TM, D), lambda i, *_: (i, 0)),
            scratch_shapes=[pltpu.VMEM((TM * DC, LANES), F32)]),
        out_shape=jax.ShapeDtypeStruct((R_MAX, D), BF16),
        compiler_params=_cparams(), name="moe_gather",
    )(tok8, tile_j0, n_tiles, h_rows)


def _experts_kernel(layer, tok8_ref, w_ref, te_ref, j0_ref, cnt_ref, first_ref, slot_ref, next_ref, nt_ref,
                    x_ref, wgu_hbm, bgu_ref, wdn_hbm, bdn_ref, acc_ref, y_ref, wgu_buf, wdn_buf, sem):
    i = pl.program_id(0)

    def weight_copies(e, slot):
        return (pltpu.make_async_copy(wgu_hbm.at[layer, e], wgu_buf.at[slot], sem.at[0, slot]),
                pltpu.make_async_copy(wdn_hbm.at[layer, e], wdn_buf.at[slot], sem.at[1, slot]))

    @pl.when(i == 0)
    def _():
        for cp in weight_copies(te_ref[0], 0):
            cp.start()
        acc_ref[...] = jnp.zeros_like(acc_ref)
        y_ref[...] = jnp.zeros_like(y_ref)

    live = i < nt_ref[0]
    slot = slot_ref[i]

    def scatter_previous_tile():
        p = jnp.maximum(i - 1, 0)
        j0 = j0_ref[p]
        cnt = jnp.where(i > 0, cnt_ref[p], 0)
        ys = y_ref.at[(i + 1) % 2]
        for r0 in range(0, TM, SCAT):
            offs, ws = [], []
            for j in range(SCAT):
                real = r0 + j < cnt
                offs.append(pl.multiple_of(jnp.where(real, tok8_ref[j0 + r0 + j], N * DC), DC))
                ws.append(jnp.where(real, w_ref[j0 + r0 + j], 0.0))
            olds = [acc_ref[pl.ds(o, DC), :] for o in offs]
            news = [olds[j] + ws[j] * ys[(r0 + j) * DC:(r0 + j + 1) * DC, :] for j in range(SCAT)]
            for j in range(SCAT):
                acc_ref[pl.ds(offs[j], DC), :] = news[j]

    @pl.when(jnp.logical_and(live, first_ref[i] == 1))
    def _():
        for cp in weight_copies(te_ref[i], slot):
            cp.wait()

        @pl.when(next_ref[i] >= 0)
        def _():
            for cp in weight_copies(next_ref[i], 1 - slot):
                cp.start()

    @pl.when(live)
    def _():
        scatter_previous_tile()
        x = x_ref[...]
        y = jnp.broadcast_to(bdn_ref[...], (TM, D))
        for f in range(0, DFF, FC):
            wg = wgu_buf[slot, :, f:f + FC].astype(BF16)
            wu = wgu_buf[slot, :, DFF + f:DFF + f + FC].astype(BF16)
            gate = jnp.dot(x, wg, preferred_element_type=F32) + bgu_ref[:, f:f + FC]
            up = jnp.dot(x, wu, preferred_element_type=F32) + bgu_ref[:, DFF + f:DFF + f + FC]
            gate = jnp.minimum(gate, LIMIT)
            up = jnp.clip(up, -LIMIT, LIMIT)
            act = (up + 1.0) * (gate * _sigmoid(ALPHA * gate))
            y = y + jnp.dot(act.astype(BF16), wdn_buf[slot, f:f + FC, :].astype(BF16), preferred_element_type=F32)
        _std_to_rows(y_ref.at[i % 2], y, TM)

    @pl.when(i == nt_ref[0])
    def _():
        scatter_previous_tile()


def _experts(layer, x_sorted, tables, w_gate_up, b_gate_up, w_down, b_down):
    tile = lambda i, *s: (jnp.minimum(i, s[-1][0] - 1), 0)
    bias = lambda i, *s: (layer, s[2][i], 0, 0)
    return pl.pallas_call(
        functools.partial(_experts_kernel, layer),
        grid_spec=pltpu.PrefetchScalarGridSpec(
            num_scalar_prefetch=len(tables), grid=(NT_MAX + 1,),
            in_specs=[pl.BlockSpec((TM, D), tile),
                      pl.BlockSpec(memory_space=pl.ANY),
                      pl.BlockSpec((None, None, 1, 2 * DFF), bias),
                      pl.BlockSpec(memory_space=pl.ANY),
                      pl.BlockSpec((None, None, 1, D), bias)],
            out_specs=pl.BlockSpec(((N + SUB) * DC, LANES), lambda i, *_: (0, 0)),
            scratch_shapes=[pltpu.VMEM((2, TM * DC, LANES), F32),
                            pltpu.VMEM((2, D, 2 * DFF), F32), pltpu.VMEM((2, DFF, D), F32),
                            pltpu.SemaphoreType.DMA((2, 2))]),
        out_shape=jax.ShapeDtypeStruct(((N + SUB) * DC, LANES), F32),
        compiler_params=_cparams(VMEM_MOE), name="moe_experts",
    )(*tables, x_sorted, w_gate_up, b_gate_up.reshape(DEPTH, NE, 1, 2 * DFF), w_down,
      b_down.reshape(DEPTH, NE, 1, D))


def _final_kernel(x_ref, moe_ref, mod_ref, g_ref, o_ref):
    x = x_ref[...] + mod_ref[:, 5 * D:6 * D] * _rows_to_std(moe_ref, TB)
    ms = jnp.mean(x * x, axis=-1, keepdims=True)
    o_ref[...] = (x * lax.rsqrt(ms + EPS)) * g_ref[...]


def _final(x, moe, mod, g):
    row = lambda i: (i, 0)
    return pl.pallas_call(
        _final_kernel,
        grid=(NB,),
        in_specs=[pl.BlockSpec((TB, D), row), pl.BlockSpec((TB * DC, LANES), row),
                  pl.BlockSpec((None, 1, NMOD * D), lambda i: (_mod_row(i), 0, 0)), _const_spec((1, D))],
        out_specs=pl.BlockSpec((TB, D), row),
        out_shape=jax.ShapeDtypeStruct((N, D), F32),
        compiler_params=_cparams(), name="final_norm",
    )(x, moe, mod, g)


def _grid_pos_embed(rows):
    quarter = D // 4
    freqs = jnp.exp(-math.log(10000.0) * jnp.arange(quarter, dtype=F32) / quarter)
    r = jnp.repeat(jnp.arange(rows, dtype=F32), GRID_W)
    col = jnp.tile(jnp.arange(GRID_W, dtype=F32), rows)
    ar = r[:, None] * freqs
    ac = col[:, None] * freqs
    return jnp.concatenate([jnp.sin(ar), jnp.cos(ar), jnp.sin(ac), jnp.cos(ac)], axis=-1)


def kernel(x_prompt, x_sample, state_ssm_re, state_ssm_im, c, c_ctx, w_mod, b_mod, norm1_g, norm2_g, w_in, b_in, w_four, conv_dw, conv_dw_b, conv_ln_g, conv_ln_b, w_conv_out, ssm_lam_re, ssm_lam_im, ssm_log_dt, ssm_b_re, ssm_b_im, ssm_c_re, ssm_c_im, ssm_d, w_ssm_glu, b_ssm_glu, w_ssm_out, w_out, router_w, router_b, w_gate_up, b_gate_up, w_down, b_down, final_norm_g):
    xs = x_sample + _grid_pos_embed(DEC_SEQ // GRID_W)[None]
    x = jnp.concatenate([x_prompt.reshape(N_CTX, D), xs.reshape(N_LAT, D)], axis=0)

    cond8 = jnp.concatenate([c_ctx[None, :], c, jnp.zeros((SUB - 1 - DEC_BATCH, D), F32)], axis=0)
    mod_all = _adaln(cond8, w_mod, b_mod)[:, :1 + DEC_BATCH].reshape(DEPTH, 1 + DEC_BATCH, 1, NMOD * D)

    w_in_b, w_four_b, w_conv_b = w_in.astype(BF16), w_four.astype(BF16), w_conv_out.astype(BF16)
    w_glu_b, w_sso_b, w_out_b = w_ssm_glu.astype(BF16), w_ssm_out.astype(BF16), w_out.astype(BF16)
    rw = jnp.pad(router_w, ((0, 0), (0, 0), (0, LANES - NE)))
    rw_hi = rw.astype(BF16)
    rw_lo = (rw - rw_hi.astype(F32)).astype(BF16)
    rb = jnp.pad(router_b, ((0, 0), (0, LANES - NE)), constant_values=-1e30).reshape(DEPTH, 1, LANES)
    r3 = lambda a: a.reshape(DEPTH, 1, a.shape[-1])

    m_all, ws_all, vs_all, a16_all = jax.vmap(_ssm_params)(
        ssm_lam_re, ssm_lam_im, ssm_log_dt, ssm_b_re, ssm_b_im, ssm_c_re, ssm_c_im, ssm_d)
    h0_all = jnp.stack([state_ssm_re[:, :, 0], state_ssm_im[:, :, 0], state_ssm_re[:, :, 1], state_ssm_im[:, :, 1]],
                       axis=0)
    h0_all = h0_all.reshape(4, DEC_BATCH, DEPTH, NPAIR, LANES).transpose(2, 0, 3, 1, 4)

    fins = []
    moe = None
    mod_prev = None
    for l in range(DEPTH):
        mod = mod_all[l]
        x, zf, v, u, gates = _inproj(l, x, moe, mod_prev, mod, r3(norm1_g), w_in_b, r3(b_in))
        brf_ctx = _fourier(l, zf, w_four_b, SEQ, 0, BATCH)
        brf_lat = _fourier(l, zf, w_four_b, DEC_SEQ, N_CTX // DEC_SEQ, DEC_BATCH)
        brc = _conv(l, v, conv_dw, r3(conv_dw_b), r3(conv_ln_g), r3(conv_ln_b), w_conv_b)
        ys, fin = _ssm(l, u, m_all, ws_all, vs_all, a16_all, h0_all)
        fins.append(fin)

        x, h_rows, top_i, top_p = _merge(l, x, brf_ctx, brf_lat, brc, ys, gates, mod, w_glu_b, r3(b_ssm_glu),
                                         w_sso_b, w_out_b, r3(norm2_g), rw_hi, rw_lo, rb)
        tables = _routing_tables(top_i, top_p)
        x_sorted = _gather(h_rows, tables[0], tables[3], tables[-1])
        moe = _experts(l, x_sorted, tables, w_gate_up, b_gate_up, w_down, b_down)
        mod_prev = mod

    y = _final(x, moe, mod_prev, final_norm_g.reshape(1, D))
    fin = jnp.stack(fins, axis=0).transpose(3, 0, 1, 2, 4).reshape(BATCH, DEPTH, 2, 2, SG, SP)
    new_re = fin[:, :, :, 0]
    new_im = fin[:, :, :, 1]
    return (y[:N_CTX].reshape(BATCH, SEQ, D), y[N_CTX:].reshape(DEC_BATCH, DEC_SEQ, D), new_re, new_im)
```

```python
import functools
import math

import numpy as np
import jax
import jax.numpy as jnp
from jax import lax
from jax.experimental import pallas as pl
from jax.experimental.pallas import tpu as pltpu

F32 = jnp.float32
BF16 = jnp.bfloat16
HI = lax.Precision.HIGHEST

D = 1024
BATCH, SEQ = 16, 256
DEC_BATCH, DEC_SEQ = 2, 1024
DEPTH = 4
N_CTX = BATCH * SEQ
N_LAT = DEC_BATCH * DEC_SEQ
N = N_CTX + N_LAT
GRID_W = 64
FW, FG = 512, 4
FGD = FW // FG
CC, TAPS = 512, 31
PAD = TAPS // 2
SW, SH = 512, 16
SG = SW // SH
SP = 64
IN_COLS = FW + 2 * CC + SW + 3 * D
NE, TOPK, DFF = 32, 4, 1024
LIMIT, ALPHA = 7.0, 1.702
EPS = 1e-6
NMOD = 6

TB = 256
NB = N // TB
CTX_BLOCKS = N_CTX // TB
LAT_BLOCKS_PER_SEQ = DEC_SEQ // TB
CH = 16
NCH = N // CH
NCH_CTX = N_CTX // CH
CPS_CTX = SEQ // CH
CPS_LAT = DEC_SEQ // CH
NPAIR = SG // 2
TM = 256
NT_MAX = (N * TOPK + NE * (TM - 1) + TM - 1) // TM
R_MAX = NT_MAX * TM
SUB = 8
LANES = 128
DC = D // LANES
SCAT = 16
FC = 1024
VMEM_DEFAULT = 48 * 1024 * 1024
VMEM_MOE = 62 * 1024 * 1024


def _cparams(vmem=VMEM_DEFAULT, ndim=1):
    return pltpu.CompilerParams(dimension_semantics=("arbitrary",) * ndim, vmem_limit_bytes=vmem)


def _const_spec(shape, layer=None):
    if layer is None:
        zeros = (0,) * len(shape)
        return pl.BlockSpec(shape, lambda *_: zeros)
    zeros = (0,) * len(shape)
    return pl.BlockSpec((None,) + tuple(shape), lambda *_: (layer,) + zeros, pipeline_mode=pl.Buffered(1))


def _mod_row(i):
    return jnp.where(i < CTX_BLOCKS, 0, 1 + (i - CTX_BLOCKS) // LAT_BLOCKS_PER_SEQ)


def _sigmoid(x):
    return 1.0 / (1.0 + jnp.exp(-x))


def _rows_to_std(ref, rows):
    return jnp.concatenate([ref[pl.ds(c, rows, stride=DC), :] for c in range(DC)], axis=1)


def _std_to_rows(ref, val, rows):
    for c in range(DC):
        ref[pl.ds(c, rows, stride=DC), :] = val[:, c * LANES:(c + 1) * LANES]


NM = SW // LANES
GPV = LANES // SH
CPB = TB // CH


def _lane_group():
    return lax.broadcasted_iota(jnp.int32, (CPB, LANES), 1) // SH


def _tokens_to_chunks(u, slab_ref, out_ref):
    for m in range(NM):
        slab_ref[m] = u[:, m * LANES:(m + 1) * LANES]
    grp = _lane_group()
    for m in range(NM):
        for kk in range(CH // GPV):
            zs = [slab_ref[m, pl.ds(kk * GPV + k8, CPB, stride=CH), :] for k8 in range(GPV)]
            for r in range(GPV):
                acc = None
                for k8 in range(GPV):
                    sh = ((k8 - r) % GPV) * SH
                    piece = zs[k8] if sh == 0 else pltpu.roll(zs[k8], sh, 1)
                    acc = piece if acc is None else jnp.where(grp == k8, piece, acc)
                out_ref[m * GPV + r, :, kk * LANES:(kk + 1) * LANES] = acc.astype(out_ref.dtype)


def _chunks_to_tokens(y_ref, slab_ref):
    grp = _lane_group()
    for m in range(NM):
        for kk in range(CH // GPV):
            ys = [y_ref[m * GPV + r, :, kk * LANES:(kk + 1) * LANES] for r in range(GPV)]
            for t8 in range(GPV):
                acc = None
                for r in range(GPV):
                    sh = ((r - t8) % GPV) * SH
                    piece = ys[r] if sh == 0 else pltpu.roll(ys[r], sh, 1)
                    acc = piece if acc is None else jnp.where(grp == r, piece, acc)
                slab_ref[m, pl.ds(kk * GPV + t8, CPB, stride=CH), :] = acc
    return jnp.concatenate([slab_ref[m] for m in range(NM)], axis=1)


MOD_COLS = 1536


def _mod_kernel(cond_ref, w_ref, b_ref, o_ref):
    c = cond_ref[...]
    s = c * _sigmoid(c)
    o_ref[...] = jnp.dot(s.astype(BF16), w_ref[...].astype(BF16), preferred_element_type=F32) + b_ref[...]


def _adaln(cond8, w_mod, b_mod):
    nc = NMOD * D // MOD_COLS
    return pl.pallas_call(
        _mod_kernel,
        grid=(DEPTH, nc),
        in_specs=[
            pl.BlockSpec((SUB, D), lambda l, j: (0, 0)),
            pl.BlockSpec((None, D, MOD_COLS), lambda l, j: (l, 0, j)),
            pl.BlockSpec((None, 1, MOD_COLS), lambda l, j: (l, 0, j)),
        ],
        out_specs=pl.BlockSpec((None, SUB, MOD_COLS), lambda l, j: (l, 0, j)),
        out_shape=jax.ShapeDtypeStruct((DEPTH, SUB, NMOD * D), F32),
        compiler_params=_cparams(ndim=2),
        name="adaln_mod",
    )(cond8, w_mod, b_mod.reshape(DEPTH, 1, NMOD * D))


def _inproj_kernel(has_moe, *refs):
    if has_moe:
        (x_ref, moe_ref, modp_ref, mod_ref, g_ref, w_ref, b_ref,
         xo_ref, zf_ref, v_ref, u_ref, gate_ref, slab_ref) = refs
        x = x_ref[...] + modp_ref[:, 5 * D:6 * D] * _rows_to_std(moe_ref, TB)
        xo_ref[...] = x
    else:
        x_ref, mod_ref, g_ref, w_ref, b_ref, zf_ref, v_ref, u_ref, gate_ref, slab_ref = refs
        x = x_ref[...]
    sh1 = mod_ref[:, 0:D]
    sc1 = mod_ref[:, D:2 * D]
    ms = jnp.mean(x * x, axis=-1, keepdims=True)
    h = (x * lax.rsqrt(ms + EPS)) * g_ref[...] * (1.0 + sc1) + sh1
    z = jnp.dot(h.astype(BF16), w_ref[...], preferred_element_type=F32) + b_ref[...]
    o = 0
    zf_ref[...] = z[:, o:o + FW].astype(BF16)
    o += FW
    za = z[:, o:o + CC]
    zb = z[:, o + CC:o + 2 * CC]
    v_ref[...] = (za * _sigmoid(zb)).astype(BF16)
    o += 2 * CC
    _tokens_to_chunks(z[:, o:o + SW], slab_ref, u_ref)
    o += SW
    gate_ref[...] = _sigmoid(z[:, o:]).astype(BF16)


def _inproj(layer, x, moe, mod_prev, mod, norm_g, w_in, b_in):
    has_moe = moe is not None
    row = lambda i: (i, 0)
    mod_spec = pl.BlockSpec((None, 1, NMOD * D), lambda i: (_mod_row(i), 0, 0))
    in_specs = [pl.BlockSpec((TB, D), row)]
    args = [x]
    if has_moe:
        in_specs += [pl.BlockSpec((TB * DC, LANES), row), mod_spec]
        args += [moe, mod_prev]
    in_specs += [mod_spec, _const_spec((1, D), layer), _const_spec((D, IN_COLS), layer),
                 _const_spec((1, IN_COLS), layer)]
    args += [mod, norm_g, w_in, b_in]
    out_specs = [pl.BlockSpec((TB, FW), row), pl.BlockSpec((TB, CC), row),
                 pl.BlockSpec((SG, CPB, CH * SH), lambda i: (0, i, 0)), pl.BlockSpec((TB, 3 * D), row)]
    out_shape = [jax.ShapeDtypeStruct((N, FW), BF16), jax.ShapeDtypeStruct((N, CC), BF16),
                 jax.ShapeDtypeStruct((SG, NCH, CH * SH), BF16), jax.ShapeDtypeStruct((N, 3 * D), BF16)]
    if has_moe:
        out_specs = [pl.BlockSpec((TB, D), row)] + out_specs
        out_shape = [jax.ShapeDtypeStruct((N, D), F32)] + out_shape
    outs = pl.pallas_call(
        functools.partial(_inproj_kernel, has_moe),
        grid=(NB,), in_specs=in_specs, out_specs=out_specs, out_shape=out_shape,
        scratch_shapes=[pltpu.VMEM((NM, TB, LANES), F32)],
        compiler_params=_cparams(), name=f"inproj_{int(has_moe)}",
    )(*args)
    if has_moe:
        return outs
    return [x] + list(outs)


def _fourier_kernel(scale, zf_ref, csbd_ref, csl_ref, wf_ref, o_ref):
    t = jnp.dot(zf_ref[...], csbd_ref[...], preferred_element_type=F32).astype(BF16)
    ts = jnp.concatenate([t[:, :FW], t[:, FW:]], axis=0)
    y = jnp.dot(csl_ref[...], ts, preferred_element_type=F32) * scale
    o_ref[...] = jnp.dot(y.astype(BF16), wf_ref[...], preferred_element_type=F32).astype(BF16)


def _dft_consts(L):
    k = np.arange(L)
    ang = 2.0 * np.pi * ((k[:, None] * k[None, :]) % L) / L
    csl = np.concatenate([np.cos(ang), -np.sin(ang)], axis=1)
    m = np.arange(FGD)
    angc = 2.0 * np.pi * ((m[:, None] * m[None, :]) % FGD) / FGD
    cbd = np.kron(np.eye(FG), np.cos(angc))
    sbd = np.kron(np.eye(FG), np.sin(angc))
    csbd = np.concatenate([cbd, sbd], axis=1)
    return jnp.asarray(csl, F32).astype(BF16), jnp.asarray(csbd, F32).astype(BF16)


def _fourier(layer, zf, w_four, L, first_block, nseq):
    csl, csbd = _dft_consts(L)
    scale = 1.0 / math.sqrt(L * FGD)
    return pl.pallas_call(
        functools.partial(_fourier_kernel, scale),
        grid=(nseq,),
        in_specs=[pl.BlockSpec((L, FW), lambda i: (i + first_block, 0)),
                  _const_spec((FW, 2 * FW)), _const_spec((L, 2 * L)), _const_spec((FW, D), layer)],
        out_specs=pl.BlockSpec((L, D), lambda i: (i, 0)),
        out_shape=jax.ShapeDtypeStruct((nseq * L, D), BF16),
        compiler_params=_cparams(), name=f"fourier_{L}",
    )(zf, csbd, csl, w_four)


HALO = 16
CONV_ROWS = 32


def _conv_kernel(vp_ref, vc_ref, vn_ref, cw_ref, cb_ref, lg_ref, lb_ref, wc_ref, o_ref, pad_ref, act_ref):
    i = pl.program_id(0)
    pos = (i - CTX_BLOCKS) % LAT_BLOCKS_PER_SEQ
    lat = i >= CTX_BLOCKS
    has_prev = jnp.logical_and(lat, pos != 0)
    has_next = jnp.logical_and(lat, pos != LAT_BLOCKS_PER_SEQ - 1)
    prev = vp_ref[TB - HALO:, :].astype(F32)
    nxt = vn_ref[:HALO, :].astype(F32)
    pad_ref[0:HALO, :] = jnp.where(has_prev, prev, 0.0)
    pad_ref[HALO:HALO + TB, :] = vc_ref[...].astype(F32)
    pad_ref[HALO + TB:, :] = jnp.where(has_next, nxt, 0.0)
    for r0 in range(0, TB, CONV_ROWS):
        acc = jnp.broadcast_to(cb_ref[...], (CONV_ROWS, CC))
        for s in range(SUB):
            part = None
            for q in range((TAPS + SUB) // SUB):
                o = SUB * q + s - (HALO - PAD)
                if 0 <= o < TAPS:
                    term = pad_ref[r0 + SUB * q:r0 + SUB * q + CONV_ROWS + SUB, :] * cw_ref[o:o + 1, :]
                    part = term if part is None else part + term
            acc = acc + part[s:s + CONV_ROWS, :]
        mu = jnp.mean(acc, axis=-1, keepdims=True)
        xc = acc - mu
        var = jnp.mean(xc * xc, axis=-1, keepdims=True)
        y = xc * lax.rsqrt(var + EPS) * lg_ref[...] + lb_ref[...]
        act_ref[r0:r0 + CONV_ROWS, :] = (y * _sigmoid(y)).astype(BF16)
    o_ref[...] = jnp.dot(act_ref[...], wc_ref[...], preferred_element_type=F32).astype(BF16)


def _conv(layer, v, conv_dw, conv_b, ln_g, ln_b, w_conv_out):
    return pl.pallas_call(
        _conv_kernel,
        grid=(NB,),
        in_specs=[pl.BlockSpec((TB, CC), lambda i: (jnp.maximum(i - 1, 0), 0)),
                  pl.BlockSpec((TB, CC), lambda i: (i, 0)),
                  pl.BlockSpec((TB, CC), lambda i: (jnp.minimum(i + 1, NB - 1), 0)),
                  _const_spec((TAPS, CC), layer), _const_spec((1, CC), layer), _const_spec((1, CC), layer),
                  _const_spec((1, CC), layer), _const_spec((CC, D), layer)],
        out_specs=pl.BlockSpec((TB, D), lambda i: (i, 0)),
        out_shape=jax.ShapeDtypeStruct((N, D), BF16),
        scratch_shapes=[pltpu.VMEM((TB + 2 * HALO, CC), F32), pltpu.VMEM((TB, CC), BF16)],
        compiler_params=_cparams(), name="conv_branch",
    )(v, v, v, conv_dw, conv_b, ln_g, ln_b, w_conv_out)


def _ssm_params(lam_re, lam_im, log_dt, b_re, b_im, c_re, c_im, d):
    dt = jnp.exp(log_dt)[..., None]
    ldr, ldi = lam_re * dt, lam_im * dt
    j = jnp.arange(CH + 1, dtype=F32)
    mag = jnp.exp(ldr[..., None] * j)
    pr, pi = mag * jnp.cos(ldi[..., None] * j), mag * jnp.sin(ldi[..., None] * j)
    nr, ni = pr[..., 1] - 1.0, pi[..., 1]
    den = lam_re * lam_re + lam_im * lam_im
    qr, qi = (nr * lam_re + ni * lam_im) / den, (ni * lam_re - nr * lam_im) / den
    bbr = qr[..., None] * b_re - qi[..., None] * b_im
    bbi = qr[..., None] * b_im + qi[..., None] * b_re
    xr = pr[..., :CH, None] * bbr[..., None, :] - pi[..., :CH, None] * bbi[..., None, :]
    xi = pr[..., :CH, None] * bbi[..., None, :] + pi[..., :CH, None] * bbr[..., None, :]
    kk = (jnp.einsum("dghp,dgpji->dgjhi", c_re, xr, precision=HI)
          - jnp.einsum("dghp,dgpji->dgjhi", c_im, xi, precision=HI))
    kf = jnp.transpose(kk[0], (0, 3, 1, 2))
    kb = jnp.transpose(kk[1], (0, 3, 1, 2))
    eye_h = np.eye(SH, dtype=np.float32)
    centre = kf[:, :, 0:1, :] + kb[:, :, 0:1, :] + d.reshape(SG, 1, 1, SH) * eye_h[None, :, None, :]
    lags = jnp.concatenate([kb[:, :, :0:-1, :], centre, kf[:, :, 1:, :], jnp.zeros((SG, SH, 1, SH), F32)], axis=2)
    m = lags.reshape(SG, SH, 2 * CH * SH)
    wfr = jnp.transpose(xr[0][:, :, ::-1, :], (0, 2, 3, 1)).reshape(SG, CH * SH, SP)
    wfi = jnp.transpose(xi[0][:, :, ::-1, :], (0, 2, 3, 1)).reshape(SG, CH * SH, SP)
    wbr = jnp.transpose(xr[1], (0, 2, 3, 1)).reshape(SG, CH * SH, SP)
    wbi = jnp.transpose(xi[1], (0, 2, 3, 1)).reshape(SG, CH * SH, SP)
    ctr, cti = jnp.transpose(c_re, (0, 1, 3, 2)), jnp.transpose(c_im, (0, 1, 3, 2))
    pfr, pfi = pr[0][..., 1:], pi[0][..., 1:]
    pbr, pbi = pr[1][..., 1:][..., ::-1], pi[1][..., 1:][..., ::-1]
    afr = ctr[0][:, :, None, :] * pfr[..., None] - cti[0][:, :, None, :] * pfi[..., None]
    afi = ctr[0][:, :, None, :] * pfi[..., None] + cti[0][:, :, None, :] * pfr[..., None]
    abr = ctr[1][:, :, None, :] * pbr[..., None] - cti[1][:, :, None, :] * pbi[..., None]
    abi = ctr[1][:, :, None, :] * pbi[..., None] + cti[1][:, :, None, :] * pbr[..., None]
    ws = jnp.concatenate([w.astype(BF16) for w in (wfr, wfi, wbr, wbi)], axis=2)
    vs = jnp.concatenate([a.reshape(SG, SP, CH * SH).astype(BF16) for a in (afr, -afi, abr, -abi)], axis=1)
    a16 = jnp.stack([pr[0][..., CH], pi[0][..., CH], pr[1][..., CH], pi[1][..., CH]], 0)
    a16 = a16.reshape(4, NPAIR, 1, LANES)
    return m, ws, vs, a16


TOE_GROUPS = 8


def _toeplitz_kernel(c_ref, o_ref):
    for g in range(TOE_GROUPS):
        cm = c_ref[g]
        for k in range(CH):
            off = (CH - 1 - k) * SH
            o_ref[g, k * SH:(k + 1) * SH, :] = cm[:, off:off + CH * SH].astype(BF16)


def _toeplitz(lags):
    ng = lags.shape[0]
    return pl.pallas_call(
        _toeplitz_kernel,
        grid=(ng // TOE_GROUPS,),
        in_specs=[pl.BlockSpec((TOE_GROUPS, SH, 2 * CH * SH), lambda i: (i, 0, 0))],
        out_specs=pl.BlockSpec((TOE_GROUPS, CH * SH, CH * SH), lambda i: (i, 0, 0)),
        out_shape=jax.ShapeDtypeStruct((ng, CH * SH, CH * SH), BF16),
        compiler_params=_cparams(), name="s5_toeplitz",
    )(lags)


def _ssm_kernel(u_ref, m_ref, ws_ref, vs_ref, a_ref, h0_ref, y_ref, fin_ref, s_ref, ef_ref):
    u0, u1 = u_ref[0], u_ref[1]
    s0 = jnp.dot(u0, ws_ref[0], preferred_element_type=F32)
    s1 = jnp.dot(u1, ws_ref[1], preferred_element_type=F32)
    for k in range(4):
        s_ref[k] = jnp.concatenate([s0[:, k * SP:(k + 1) * SP], s1[:, k * SP:(k + 1) * SP]], axis=1)

    def scan(kr, ki, first, nrows, nsteps, reverse, er, ei):
        ar, ai = a_ref[kr], a_ref[ki]
        order = range(nsteps - 1, -1, -1) if reverse else range(nsteps)
        for c in order:
            rows = pl.ds(first + c, nrows, stride=nsteps)
            ef_ref[kr, rows, :] = er
            ef_ref[ki, rows, :] = ei
            sr, si = s_ref[kr, rows, :], s_ref[ki, rows, :]
            er, ei = ar * er - ai * ei + sr, ar * ei + ai * er + si
        return er, ei

    zc = jnp.zeros((BATCH, LANES), F32)
    for kr, ki, rev in ((0, 1, False), (2, 3, True)):
        er, ei = scan(kr, ki, 0, BATCH, CPS_CTX, rev, zc, zc)
        fin_ref[kr] = er
        fin_ref[ki] = ei
        scan(kr, ki, NCH_CTX, DEC_BATCH, CPS_LAT, rev, h0_ref[kr], h0_ref[ki])

    for q, uq in ((0, u0), (1, u1)):
        ef = jnp.concatenate([ef_ref[k, :, q * SP:(q + 1) * SP] for k in range(4)], axis=1).astype(BF16)
        y_ref[q] = (jnp.dot(uq, m_ref[q], preferred_element_type=F32)
                    + jnp.dot(ef, vs_ref[q], preferred_element_type=F32))


def _ssm(layer, u_chunks, m, ws, vs, a16, h0):
    pair = lambda j: (j, 0, 0)
    return pl.pallas_call(
        _ssm_kernel,
        grid=(NPAIR,),
        in_specs=[pl.BlockSpec((2, NCH, CH * SH), pair),
                  pl.BlockSpec((None, 2, CH * SH, CH * SH), lambda j: (layer, j, 0, 0)),
                  pl.BlockSpec((None, 2, CH * SH, 4 * SP), lambda j: (layer, j, 0, 0)),
                  pl.BlockSpec((None, 2, 4 * SP, CH * SH), lambda j: (layer, j, 0, 0)),
                  pl.BlockSpec((None, 4, None, 1, LANES), lambda j: (layer, 0, j, 0, 0)),
                  pl.BlockSpec((None, 4, None, DEC_BATCH, LANES), lambda j: (layer, 0, j, 0, 0))],
        out_specs=[pl.BlockSpec((2, NCH, CH * SH), pair),
                   pl.BlockSpec((4, None, BATCH, LANES), lambda j: (0, j, 0, 0))],
        out_shape=[jax.ShapeDtypeStruct((SG, NCH, CH * SH), F32),
                   jax.ShapeDtypeStruct((4, NPAIR, BATCH, LANES), F32)],
        scratch_shapes=[pltpu.VMEM((4, NCH, LANES), F32), pltpu.VMEM((4, NCH, LANES), F32)],
        compiler_params=_cparams(), name="s5_chunked",
    )(u_chunks, m, ws, vs, a16, h0)


def _merge_kernel(x_ref, bfc_ref, bfl_ref, bc_ref, ys_ref, gate_ref, mod_ref, wg_ref, bg_ref, wso_ref, wo_ref,
                  n2_ref, rwh_ref, rwl_ref, rb_ref, xo_ref, h_ref, ti_ref, tp_ref, slab_ref):
    i = pl.program_id(0)
    g = jax.nn.gelu(_chunks_to_tokens(ys_ref, slab_ref))
    gl = jnp.dot(g.astype(BF16), wg_ref[...], preferred_element_type=F32) + bg_ref[...]
    y2 = g * _sigmoid(gl)
    brs = jnp.dot(y2.astype(BF16), wso_ref[...], preferred_element_type=F32)
    brf = jnp.where(i < CTX_BLOCKS, bfc_ref[...], bfl_ref[...]).astype(F32)
    mixed = (gate_ref[:, 0:D].astype(F32) * brf
             + gate_ref[:, D:2 * D].astype(F32) * bc_ref[...].astype(F32)
             + gate_ref[:, 2 * D:3 * D].astype(F32) * brs)
    g1 = mod_ref[:, 2 * D:3 * D]
    x = x_ref[...] + g1 * jnp.dot(mixed.astype(BF16), wo_ref[...], preferred_element_type=F32)
    xo_ref[...] = x
    sh2 = mod_ref[:, 3 * D:4 * D]
    sc2 = mod_ref[:, 4 * D:5 * D]
    ms = jnp.mean(x * x, axis=-1, keepdims=True)
    h = (x * lax.rsqrt(ms + EPS)) * n2_ref[...] * (1.0 + sc2) + sh2
    _std_to_rows(h_ref, h, TB)
    hh = h.astype(BF16)
    hl = (h - hh.astype(F32)).astype(BF16)
    logits = (jnp.dot(hh, rwh_ref[...], preferred_element_type=F32)
              + jnp.dot(hl, rwh_ref[...], preferred_element_type=F32)
              + jnp.dot(hh, rwl_ref[...], preferred_element_type=F32)) + rb_ref[...]
    lane = lax.broadcasted_iota(jnp.int32, (TB, LANES), 1).astype(F32)
    vals, idxs = [], []
    l = logits
    for _ in range(TOPK):
        m = jnp.max(l, axis=-1, keepdims=True)
        idx = jnp.min(jnp.where(l == m, lane, float(LANES)), axis=-1, keepdims=True)
        vals.append(m)
        idxs.append(idx)
        l = jnp.where(lane == idx, -jnp.inf, l)
    es = [jnp.exp(v - vals[0]) for v in vals]
    tot = es[0] + es[1] + es[2] + es[3]
    ti = jnp.zeros((TB, LANES), F32)
    tp = jnp.zeros((TB, LANES), F32)
    for k in range(TOPK):
        ti = jnp.where(lane == float(k), idxs[k], ti)
        tp = jnp.where(lane == float(k), es[k] / tot, tp)
    ti_ref[...] = ti.astype(jnp.int32)
    tp_ref[...] = tp


def _merge(layer, x, brf_ctx, brf_lat, brc, ys, gates, mod, w_glu, b_glu, w_ssm_out, w_out, norm2_g,
           rw_hi, rw_lo, rb):
    row = lambda i: (i, 0)
    return pl.pallas_call(
        _merge_kernel,
        grid=(NB,),
        in_specs=[pl.BlockSpec((TB, D), row),
                  pl.BlockSpec((TB, D), lambda i: (jnp.minimum(i, CTX_BLOCKS - 1), 0)),
                  pl.BlockSpec((TB, D), lambda i: (jnp.maximum(i - CTX_BLOCKS, 0), 0)),
                  pl.BlockSpec((TB, D), row),
                  pl.BlockSpec((SG, CPB, CH * SH), lambda i: (0, i, 0)),
                  pl.BlockSpec((TB, 3 * D), row),
                  pl.BlockSpec((None, 1, NMOD * D), lambda i: (_mod_row(i), 0, 0)),
                  _const_spec((SW, SW), layer), _const_spec((1, SW), layer), _const_spec((SW, D), layer),
                  _const_spec((D, D), layer), _const_spec((1, D), layer),
                  _const_spec((D, LANES), layer), _const_spec((D, LANES), layer), _const_spec((1, LANES), layer)],
        out_specs=[pl.BlockSpec((TB, D), row), pl.BlockSpec((TB * DC, LANES), row),
                   pl.BlockSpec((TB, LANES), row), pl.BlockSpec((TB, LANES), row)],
        out_shape=[jax.ShapeDtypeStruct((N, D), F32), jax.ShapeDtypeStruct((N * DC, LANES), F32),
                   jax.ShapeDtypeStruct((N, LANES), jnp.int32), jax.ShapeDtypeStruct((N, LANES), F32)],
        scratch_shapes=[pltpu.VMEM((NM, TB, LANES), F32)],
        compiler_params=_cparams(), name="merge_router",
    )(x, brf_ctx, brf_lat, brc, ys, gates, mod, w_glu, b_glu, w_ssm_out, w_out, norm2_g, rw_hi, rw_lo, rb)


def _routing_tables(top_i, top_p):
    e_flat = top_i[:, :TOPK].reshape(-1)
    p_flat = top_p[:, :TOPK].reshape(-1)
    na = N * TOPK
    key = e_flat * 32768 + jnp.arange(na, dtype=jnp.int32)
    skey, sw = lax.sort_key_val(key, p_flat)
    tok8 = ((skey & 32767) // TOPK) * DC
    tok8 = jnp.concatenate([tok8, jnp.zeros((TM,), jnp.int32)])
    sw = jnp.concatenate([sw, jnp.zeros((TM,), F32)])
    experts = jnp.arange(NE, dtype=jnp.int32)
    counts = jnp.sum((e_flat[:, None] == experts[None, :]).astype(jnp.int32), axis=0)
    ntile = (counts + TM - 1) // TM
    tend = jnp.cumsum(ntile)
    tstart = tend - ntile
    cstart = jnp.cumsum(counts) - counts
    n_tiles = tend[-1]
    later = jnp.logical_and(experts[None, :] > experts[:, None], ntile[None, :] > 0)
    nxt_e = jnp.min(jnp.where(later, experts[None, :], NE), axis=1)
    nxt_e = jnp.where(nxt_e == NE, -1, nxt_e)
    used = (ntile > 0).astype(jnp.int32)
    used_before = jnp.cumsum(used) - used
    ti = jnp.arange(NT_MAX + 1, dtype=jnp.int32)
    t = jnp.minimum(ti, n_tiles - 1)
    tile_e = jnp.minimum(jnp.sum((t[:, None] >= tend[None, :]).astype(jnp.int32), axis=1), NE - 1)
    onehot = (tile_e[:, None] == experts[None, :]).astype(jnp.int32)
    pick = lambda tab: jnp.sum(onehot * tab[None, :], axis=1)
    k = t - pick(tstart)
    tile_j0 = pick(cstart) + k * TM
    tile_cnt = jnp.clip(pick(counts) - k * TM, 0, TM)
    tile_first = jnp.logical_and(k == 0, ti < n_tiles)
    tile_slot = pick(used_before) % 2
    tile_next = pick(nxt_e)
    i32 = lambda a: a.astype(jnp.int32)
    return (tok8, sw, i32(tile_e), i32(tile_j0), i32(tile_cnt), i32(tile_first), i32(tile_slot), i32(tile_next),
            i32(n_tiles.reshape(1)))


GROWS = 16


def _gather_kernel(tok8_ref, j0_ref, nt_ref, h_ref, o_ref, g_ref):
    i = pl.program_id(0)

    @pl.when(i < nt_ref[0])
    def _():
        j0 = j0_ref[i]

        def body(b, carry):
            base = j0 + b * GROWS
            rows = [h_ref[pl.ds(pl.multiple_of(tok8_ref[base + j], DC), DC), :] for j in range(GROWS)]
            for j in range(GROWS):
                g_ref[pl.ds(pl.multiple_of((b * GROWS + j) * DC, DC), DC), :] = rows[j]
            return carry

        lax.fori_loop(0, TM // GROWS, body, 0)
        for c in range(DC):
            o_ref[:, c * LANES:(c + 1) * LANES] = g_ref[pl.ds(c, TM, stride=DC), :].astype(BF16)

    @pl.when(i >= nt_ref[0])
    def _():
        o_ref[...] = jnp.zeros_like(o_ref)


def _gather(h_rows, tok8, tile_j0, n_tiles):
    return pl.pallas_call(
        _gather_kernel,
        grid_spec=pltpu.PrefetchScalarGridSpec(
            num_scalar_prefetch=3, grid=(NT_MAX,),
            in_specs=[pl.BlockSpec((N * DC, LANES), lambda i, *_: (0, 0))],
            out_specs=pl.BlockSpec((TM, D), lambda i, *_: (i, 0)),
            scratch_shapes=[pltpu.VMEM((TM * DC, LANES), F32)]),
        out_shape=jax.ShapeDtypeStruct((R_MAX, D), BF16),
        compiler_params=_cparams(), name="moe_gather",
    )(tok8, tile_j0, n_tiles, h_rows)


def _experts_kernel(layer, tok8_ref, w_ref, te_ref, j0_ref, cnt_ref, first_ref, slot_ref, next_ref, nt_ref,
                    x_ref, wgu_hbm, bgu_ref, wdn_hbm, bdn_ref, acc_ref, y_ref, wgu_buf, wdn_buf, sem):
    i = pl.program_id(0)

    def weight_copies(e, slot):
        return (pltpu.make_async_copy(wgu_hbm.at[layer, e], wgu_buf.at[slot], sem.at[0, slot]),
                pltpu.make_async_copy(wdn_hbm.at[layer, e], wdn_buf.at[slot], sem.at[1, slot]))

    @pl.when(i == 0)
    def _():
        for cp in weight_copies(te_ref[0], 0):
            cp.start()
        acc_ref[...] = jnp.zeros_like(acc_ref)
        y_ref[...] = jnp.zeros_like(y_ref)

    live = i < nt_ref[0]
    slot = slot_ref[i]

    def scatter_previous_tile():
        p = jnp.maximum(i - 1, 0)
        j0 = j0_ref[p]
        cnt = jnp.where(i > 0, cnt_ref[p], 0)
        ys = y_ref.at[(i + 1) % 2]
        for r0 in range(0, TM, SCAT):
            offs, ws = [], []
            for j in range(SCAT):
                real = r0 + j < cnt
                offs.append(pl.multiple_of(jnp.where(real, tok8_ref[j0 + r0 + j], N * DC), DC))
                ws.append(jnp.where(real, w_ref[j0 + r0 + j], 0.0))
            olds = [acc_ref[pl.ds(o, DC), :] for o in offs]
            news = [olds[j] + ws[j] * ys[(r0 + j) * DC:(r0 + j + 1) * DC, :] for j in range(SCAT)]
            for j in range(SCAT):
                acc_ref[pl.ds(offs[j], DC), :] = news[j]

    @pl.when(jnp.logical_and(live, first_ref[i] == 1))
    def _():
        for cp in weight_copies(te_ref[i], slot):
            cp.wait()

        @pl.when(next_ref[i] >= 0)
        def _():
            for cp in weight_copies(next_ref[i], 1 - slot):
                cp.start()

    @pl.when(live)
    def _():
        scatter_previous_tile()
        x = x_ref[...]
        y = jnp.broadcast_to(bdn_ref[...], (TM, D))
        for f in range(0, DFF, FC):
            wg = wgu_buf[slot, :, f:f + FC].astype(BF16)
            wu = wgu_buf[slot, :, DFF + f:DFF + f + FC].astype(BF16)
            gate = jnp.dot(x, wg, preferred_element_type=F32) + bgu_ref[:, f:f + FC]
            up = jnp.dot(x, wu, preferred_element_type=F32) + bgu_ref[:, DFF + f:DFF + f + FC]
            gate = jnp.minimum(gate, LIMIT)
            up = jnp.clip(up, -LIMIT, LIMIT)
            act = (up + 1.0) * (gate * _sigmoid(ALPHA * gate))
            y = y + jnp.dot(act.astype(BF16), wdn_buf[slot, f:f + FC, :].astype(BF16), preferred_element_type=F32)
        _std_to_rows(y_ref.at[i % 2], y, TM)

    @pl.when(i == nt_ref[0])
    def _():
        scatter_previous_tile()


def _experts(layer, x_sorted, tables, w_gate_up, b_gate_up, w_down, b_down):
    tile = lambda i, *s: (jnp.minimum(i, s[-1][0] - 1), 0)
    bias = lambda i, *s: (layer, s[2][i], 0, 0)
    return pl.pallas_call(
        functools.partial(_experts_kernel, layer),
        grid_spec=pltpu.PrefetchScalarGridSpec(
            num_scalar_prefetch=len(tables), grid=(NT_MAX + 1,),
            in_specs=[pl.BlockSpec((TM, D), tile),
                      pl.BlockSpec(memory_space=pl.ANY),
                      pl.BlockSpec((None, None, 1, 2 * DFF), bias),
                      pl.BlockSpec(memory_space=pl.ANY),
                      pl.BlockSpec((None, None, 1, D), bias)],
            out_specs=pl.BlockSpec(((N + SUB) * DC, LANES), lambda i, *_: (0, 0)),
            scratch_shapes=[pltpu.VMEM((2, TM * DC, LANES), F32),
                            pltpu.VMEM((2, D, 2 * DFF), F32), pltpu.VMEM((2, DFF, D), F32),
                            pltpu.SemaphoreType.DMA((2, 2))]),
        out_shape=jax.ShapeDtypeStruct(((N + SUB) * DC, LANES), F32),
        compiler_params=_cparams(VMEM_MOE), name="moe_experts",
    )(*tables, x_sorted, w_gate_up, b_gate_up.reshape(DEPTH, NE, 1, 2 * DFF), w_down,
      b_down.reshape(DEPTH, NE, 1, D))


def _final_kernel(x_ref, moe_ref, mod_ref, g_ref, o_ref):
    x = x_ref[...] + mod_ref[:, 5 * D:6 * D] * _rows_to_std(moe_ref, TB)
    ms = jnp.mean(x * x, axis=-1, keepdims=True)
    o_ref[...] = (x * lax.rsqrt(ms + EPS)) * g_ref[...]


def _final(x, moe, mod, g):
    row = lambda i: (i, 0)
    return pl.pallas_call(
        _final_kernel,
        grid=(NB,),
        in_specs=[pl.BlockSpec((TB, D), row), pl.BlockSpec((TB * DC, LANES), row),
                  pl.BlockSpec((None, 1, NMOD * D), lambda i: (_mod_row(i), 0, 0)), _const_spec((1, D))],
        out_specs=pl.BlockSpec((TB, D), row),
        out_shape=jax.ShapeDtypeStruct((N, D), F32),
        compiler_params=_cparams(), name="final_norm",
    )(x, moe, mod, g)


def _grid_pos_embed(rows):
    quarter = D // 4
    freqs = jnp.exp(-math.log(10000.0) * jnp.arange(quarter, dtype=F32) / quarter)
    r = jnp.repeat(jnp.arange(rows, dtype=F32), GRID_W)
    col = jnp.tile(jnp.arange(GRID_W, dtype=F32), rows)
    ar = r[:, None] * freqs
    ac = col[:, None] * freqs
    return jnp.concatenate([jnp.sin(ar), jnp.cos(ar), jnp.sin(ac), jnp.cos(ac)], axis=-1)


def kernel(x_prompt, x_sample, state_ssm_re, state_ssm_im, c, c_ctx, w_mod, b_mod, norm1_g, norm2_g, w_in, b_in, w_four, conv_dw, conv_dw_b, conv_ln_g, conv_ln_b, w_conv_out, ssm_lam_re, ssm_lam_im, ssm_log_dt, ssm_b_re, ssm_b_im, ssm_c_re, ssm_c_im, ssm_d, w_ssm_glu, b_ssm_glu, w_ssm_out, w_out, router_w, router_b, w_gate_up, b_gate_up, w_down, b_down, final_norm_g):
    xs = x_sample + _grid_pos_embed(DEC_SEQ // GRID_W)[None]
    x = jnp.concatenate([x_prompt.reshape(N_CTX, D), xs.reshape(N_LAT, D)], axis=0)

    cond8 = jnp.concatenate([c_ctx[None, :], c, jnp.zeros((SUB - 1 - DEC_BATCH, D), F32)], axis=0)
    mod_all = _adaln(cond8, w_mod, b_mod)[:, :1 + DEC_BATCH].reshape(DEPTH, 1 + DEC_BATCH, 1, NMOD * D)

    w_in_b, w_four_b, w_conv_b = w_in.astype(BF16), w_four.astype(BF16), w_conv_out.astype(BF16)
    w_glu_b, w_sso_b, w_out_b = w_ssm_glu.astype(BF16), w_ssm_out.astype(BF16), w_out.astype(BF16)
    rw = jnp.pad(router_w, ((0, 0), (0, 0), (0, LANES - NE)))
    rw_hi = rw.astype(BF16)
    rw_lo = (rw - rw_hi.astype(F32)).astype(BF16)
    rb = jnp.pad(router_b, ((0, 0), (0, LANES - NE)), constant_values=-1e30).reshape(DEPTH, 1, LANES)
    r3 = lambda a: a.reshape(DEPTH, 1, a.shape[-1])

    lags_all, ws_all, vs_all, a16_all = jax.vmap(_ssm_params)(
        ssm_lam_re, ssm_lam_im, ssm_log_dt, ssm_b_re, ssm_b_im, ssm_c_re, ssm_c_im, ssm_d)
    m_all = _toeplitz(lags_all.reshape(DEPTH * SG, SH, 2 * CH * SH)).reshape(DEPTH, SG, CH * SH, CH * SH)
    h0_all = jnp.stack([state_ssm_re[:, :, 0], state_ssm_im[:, :, 0], state_ssm_re[:, :, 1], state_ssm_im[:, :, 1]],
                       axis=0)
    h0_all = h0_all.reshape(4, DEC_BATCH, DEPTH, NPAIR, LANES).transpose(2, 0, 3, 1, 4)

    fins = []
    moe = None
    mod_prev = None
    for l in range(DEPTH):
        mod = mod_all[l]
        x, zf, v, u, gates = _inproj(l, x, moe, mod_prev, mod, r3(norm1_g), w_in_b, r3(b_in))
        brf_ctx = _fourier(l, zf, w_four_b, SEQ, 0, BATCH)
        brf_lat = _fourier(l, zf, w_four_b, DEC_SEQ, N_CTX // DEC_SEQ, DEC_BATCH)
        brc = _conv(l, v, conv_dw, r3(conv_dw_b), r3(conv_ln_g), r3(conv_ln_b), w_conv_b)
        ys, fin = _ssm(l, u, m_all, ws_all, vs_all, a16_all, h0_all)
        fins.append(fin)

        x, h_rows, top_i, top_p = _merge(l, x, brf_ctx, brf_lat, brc, ys, gates, mod, w_glu_b, r3(b_ssm_glu),
                                         w_sso_b, w_out_b, r3(norm2_g), rw_hi, rw_lo, rb)
        tables = _routing_tables(top_i, top_p)
        x_sorted = _gather(h_rows, tables[0], tables[3], tables[-1])
        moe = _experts(l, x_sorted, tables, w_gate_up, b_gate_up, w_down, b_down)
        mod_prev = mod

    y = _final(x, moe, mod_prev, final_norm_g.reshape(1, D))
    fin = jnp.stack(fins, axis=0).transpose(3, 0, 1, 2, 4).reshape(BATCH, DEPTH, 2, 2, SG, SP)
    new_re = fin[:, :, :, 0]
    new_im = fin[:, :, :, 1]
    return (y[:N_CTX].reshape(BATCH, SEQ, D), y[N_CTX:].reshape(DEC_BATCH, DEC_SEQ, D), new_re, new_im)
```

```python
import functools
import math

import numpy as np
import jax
import jax.numpy as jnp
from jax import lax
from jax.experimental import pallas as pl
from jax.experimental.pallas import tpu as pltpu

F32 = jnp.float32
BF16 = jnp.bfloat16
HI = lax.Precision.HIGHEST

D = 1024
BATCH, SEQ = 16, 256
DEC_BATCH, DEC_SEQ = 2, 1024
DEPTH = 4
N_CTX = BATCH * SEQ
N_LAT = DEC_BATCH * DEC_SEQ
N = N_CTX + N_LAT
GRID_W = 64
FW, FG = 512, 4
FGD = FW // FG
CC, TAPS = 512, 31
PAD = TAPS // 2
SW, SH = 512, 16
SG = SW // SH
SP = 64
IN_COLS = FW + 2 * CC + SW + 3 * D
NE, TOPK, DFF = 32, 4, 1024
LIMIT, ALPHA = 7.0, 1.702
EPS = 1e-6
NMOD = 6

TB = 256
NB = N // TB
CTX_BLOCKS = N_CTX // TB
LAT_BLOCKS_PER_SEQ = DEC_SEQ // TB
CH = 16
NCH = N // CH
NCH_CTX = N_CTX // CH
CPS_CTX = SEQ // CH
CPS_LAT = DEC_SEQ // CH
NPAIR = SG // 2
TM = 256
NT_MAX = (N * TOPK + NE * (TM - 1) + TM - 1) // TM
R_MAX = NT_MAX * TM
SUB = 8
LANES = 128
DC = D // LANES
SCAT = 16
FC = 1024
VMEM_DEFAULT = 48 * 1024 * 1024
VMEM_MOE = 62 * 1024 * 1024


def _cparams(vmem=VMEM_DEFAULT, ndim=1):
    return pltpu.CompilerParams(dimension_semantics=("arbitrary",) * ndim, vmem_limit_bytes=vmem)


def _const_spec(shape, layer=None):
    if layer is None:
        zeros = (0,) * len(shape)
        return pl.BlockSpec(shape, lambda *_: zeros)
    zeros = (0,) * len(shape)
    return pl.BlockSpec((None,) + tuple(shape), lambda *_: (layer,) + zeros, pipeline_mode=pl.Buffered(1))


def _mod_row(i):
    return jnp.where(i < CTX_BLOCKS, 0, 1 + (i - CTX_BLOCKS) // LAT_BLOCKS_PER_SEQ)


def _sigmoid(x):
    return 1.0 / (1.0 + jnp.exp(-x))


def _rows_to_std(ref, rows):
    return jnp.concatenate([ref[pl.ds(c, rows, stride=DC), :] for c in range(DC)], axis=1)


def _std_to_rows(ref, val, rows):
    for c in range(DC):
        ref[pl.ds(c, rows, stride=DC), :] = val[:, c * LANES:(c + 1) * LANES]


NM = SW // LANES
GPV = LANES // SH
CPB = TB // CH


def _lane_group():
    return lax.broadcasted_iota(jnp.int32, (CPB, LANES), 1) // SH


def _tokens_to_chunks(u, slab_ref, out_ref):
    for m in range(NM):
        slab_ref[m] = u[:, m * LANES:(m + 1) * LANES]
    grp = _lane_group()
    for m in range(NM):
        for kk in range(CH // GPV):
            zs = [slab_ref[m, pl.ds(kk * GPV + k8, CPB, stride=CH), :] for k8 in range(GPV)]
            for r in range(GPV):
                acc = None
                for k8 in range(GPV):
                    sh = ((k8 - r) % GPV) * SH
                    piece = zs[k8] if sh == 0 else pltpu.roll(zs[k8], sh, 1)
                    acc = piece if acc is None else jnp.where(grp == k8, piece, acc)
                out_ref[m * GPV + r, :, kk * LANES:(kk + 1) * LANES] = acc.astype(out_ref.dtype)


def _chunks_to_tokens(y_ref, slab_ref):
    grp = _lane_group()
    for m in range(NM):
        for kk in range(CH // GPV):
            ys = [y_ref[m * GPV + r, :, kk * LANES:(kk + 1) * LANES] for r in range(GPV)]
            for t8 in range(GPV):
                acc = None
                for r in range(GPV):
                    sh = ((r - t8) % GPV) * SH
                    piece = ys[r] if sh == 0 else pltpu.roll(ys[r], sh, 1)
                    acc = piece if acc is None else jnp.where(grp == r, piece, acc)
                slab_ref[m, pl.ds(kk * GPV + t8, CPB, stride=CH), :] = acc
    return jnp.concatenate([slab_ref[m] for m in range(NM)], axis=1)


MOD_COLS = 1536


def _mod_kernel(cond_ref, w_ref, b_ref, o_ref):
    c = cond_ref[...]
    s = c * _sigmoid(c)
    o_ref[...] = jnp.dot(s.astype(BF16), w_ref[...].astype(BF16), preferred_element_type=F32) + b_ref[...]


def _adaln(cond8, w_mod, b_mod):
    nc = NMOD * D // MOD_COLS
    return pl.pallas_call(
        _mod_kernel,
        grid=(DEPTH, nc),
        in_specs=[
            pl.BlockSpec((SUB, D), lambda l, j: (0, 0)),
            pl.BlockSpec((None, D, MOD_COLS), lambda l, j: (l, 0, j)),
            pl.BlockSpec((None, 1, MOD_COLS), lambda l, j: (l, 0, j)),
        ],
        out_specs=pl.BlockSpec((None, SUB, MOD_COLS), lambda l, j: (l, 0, j)),
        out_shape=jax.ShapeDtypeStruct((DEPTH, SUB, NMOD * D), F32),
        compiler_params=_cparams(ndim=2),
        name="adaln_mod",
    )(cond8, w_mod, b_mod.reshape(DEPTH, 1, NMOD * D))


def _inproj_kernel(has_moe, *refs):
    if has_moe:
        (x_ref, moe_ref, modp_ref, mod_ref, g_ref, w_ref, b_ref,
         xo_ref, zf_ref, v_ref, u_ref, gate_ref, slab_ref) = refs
        x = x_ref[...] + modp_ref[:, 5 * D:6 * D] * _rows_to_std(moe_ref, TB)
        xo_ref[...] = x
    else:
        (xp_ref, xs_ref, pos_ref, mod_ref, g_ref, w_ref, b_ref,
         xo_ref, zf_ref, v_ref, u_ref, gate_ref, slab_ref) = refs
        x = jnp.where(pl.program_id(0) < CTX_BLOCKS, xp_ref[...], xs_ref[...] + pos_ref[...])
        xo_ref[...] = x
    sh1 = mod_ref[:, 0:D]
    sc1 = mod_ref[:, D:2 * D]
    ms = jnp.mean(x * x, axis=-1, keepdims=True)
    h = (x * lax.rsqrt(ms + EPS)) * g_ref[...] * (1.0 + sc1) + sh1
    z = jnp.dot(h.astype(BF16), w_ref[...], preferred_element_type=F32) + b_ref[...]
    o = 0
    zf_ref[...] = z[:, o:o + FW].astype(BF16)
    o += FW
    za = z[:, o:o + CC]
    zb = z[:, o + CC:o + 2 * CC]
    v_ref[...] = (za * _sigmoid(zb)).astype(BF16)
    o += 2 * CC
    _tokens_to_chunks(z[:, o:o + SW], slab_ref, u_ref)
    o += SW
    gate_ref[...] = _sigmoid(z[:, o:]).astype(BF16)


def _inproj(layer, x, moe, mod_prev, mod, norm_g, w_in, b_in):
    has_moe = moe is not None
    row = lambda i: (i, 0)
    mod_spec = pl.BlockSpec((None, 1, NMOD * D), lambda i: (_mod_row(i), 0, 0))
    if has_moe:
        in_specs = [pl.BlockSpec((TB, D), row), pl.BlockSpec((TB * DC, LANES), row), mod_spec]
        args = [x, moe, mod_prev]
    else:
        lat = lambda i: jnp.maximum(i - CTX_BLOCKS, 0)
        in_specs = [pl.BlockSpec((TB, D), lambda i: (jnp.minimum(i, CTX_BLOCKS - 1), 0)),
                    pl.BlockSpec((TB, D), lambda i: (lat(i), 0)),
                    pl.BlockSpec((TB, D), lambda i: (lat(i) % LAT_BLOCKS_PER_SEQ, 0))]
        args = list(x)
    in_specs += [mod_spec, _const_spec((1, D), layer), _const_spec((D, IN_COLS), layer),
                 _const_spec((1, IN_COLS), layer)]
    args += [mod, norm_g, w_in, b_in]
    out_specs = [pl.BlockSpec((TB, FW), row), pl.BlockSpec((TB, CC), row),
                 pl.BlockSpec((SG, CPB, CH * SH), lambda i: (0, i, 0)), pl.BlockSpec((TB, 3 * D), row)]
    out_shape = [jax.ShapeDtypeStruct((N, FW), BF16), jax.ShapeDtypeStruct((N, CC), BF16),
                 jax.ShapeDtypeStruct((SG, NCH, CH * SH), BF16), jax.ShapeDtypeStruct((N, 3 * D), BF16)]
    out_specs = [pl.BlockSpec((TB, D), row)] + out_specs
    out_shape = [jax.ShapeDtypeStruct((N, D), F32)] + out_shape
    return pl.pallas_call(
        functools.partial(_inproj_kernel, has_moe),
        grid=(NB,), in_specs=in_specs, out_specs=out_specs, out_shape=out_shape,
        scratch_shapes=[pltpu.VMEM((NM, TB, LANES), F32)],
        compiler_params=_cparams(), name=f"inproj_{int(has_moe)}",
    )(*args)


def _fourier_kernel(scale, zf_ref, csbd_ref, csl_ref, wf_ref, o_ref):
    t = jnp.dot(zf_ref[...], csbd_ref[...], preferred_element_type=F32).astype(BF16)
    ts = jnp.concatenate([t[:, :FW], t[:, FW:]], axis=0)
    y = jnp.dot(csl_ref[...], ts, preferred_element_type=F32) * scale
    o_ref[...] = jnp.dot(y.astype(BF16), wf_ref[...], preferred_element_type=F32).astype(BF16)


def _dft_consts(L):
    k = np.arange(L)
    ang = 2.0 * np.pi * ((k[:, None] * k[None, :]) % L) / L
    csl = np.concatenate([np.cos(ang), -np.sin(ang)], axis=1)
    m = np.arange(FGD)
    angc = 2.0 * np.pi * ((m[:, None] * m[None, :]) % FGD) / FGD
    cbd = np.kron(np.eye(FG), np.cos(angc))
    sbd = np.kron(np.eye(FG), np.sin(angc))
    csbd = np.concatenate([cbd, sbd], axis=1)
    return jnp.asarray(csl, F32).astype(BF16), jnp.asarray(csbd, F32).astype(BF16)


def _fourier(layer, zf, w_four, L, first_block, nseq):
    csl, csbd = _dft_consts(L)
    scale = 1.0 / math.sqrt(L * FGD)
    return pl.pallas_call(
        functools.partial(_fourier_kernel, scale),
        grid=(nseq,),
        in_specs=[pl.BlockSpec((L, FW), lambda i: (i + first_block, 0)),
                  _const_spec((FW, 2 * FW)), _const_spec((L, 2 * L)), _const_spec((FW, D), layer)],
        out_specs=pl.BlockSpec((L, D), lambda i: (i, 0)),
        out_shape=jax.ShapeDtypeStruct((nseq * L, D), BF16),
        compiler_params=_cparams(), name=f"fourier_{L}",
    )(zf, csbd, csl, w_four)


HALO = 16
CONV_ROWS = 32


def _conv_kernel(vp_ref, vc_ref, vn_ref, cw_ref, cb_ref, lg_ref, lb_ref, wc_ref, o_ref, pad_ref, act_ref):
    i = pl.program_id(0)
    pos = (i - CTX_BLOCKS) % LAT_BLOCKS_PER_SEQ
    lat = i >= CTX_BLOCKS
    has_prev = jnp.logical_and(lat, pos != 0)
    has_next = jnp.logical_and(lat, pos != LAT_BLOCKS_PER_SEQ - 1)
    prev = vp_ref[TB - HALO:, :].astype(F32)
    nxt = vn_ref[:HALO, :].astype(F32)
    pad_ref[0:HALO, :] = jnp.where(has_prev, prev, 0.0)
    pad_ref[HALO:HALO + TB, :] = vc_ref[...].astype(F32)
    pad_ref[HALO + TB:, :] = jnp.where(has_next, nxt, 0.0)
    for r0 in range(0, TB, CONV_ROWS):
        acc = jnp.broadcast_to(cb_ref[...], (CONV_ROWS, CC))
        for s in range(SUB):
            part = None
            for q in range((TAPS + SUB) // SUB):
                o = SUB * q + s - (HALO - PAD)
                if 0 <= o < TAPS:
                    term = pad_ref[r0 + SUB * q:r0 + SUB * q + CONV_ROWS + SUB, :] * cw_ref[o:o + 1, :]
                    part = term if part is None else part + term
            acc = acc + part[s:s + CONV_ROWS, :]
        mu = jnp.mean(acc, axis=-1, keepdims=True)
        xc = acc - mu
        var = jnp.mean(xc * xc, axis=-1, keepdims=True)
        y = xc * lax.rsqrt(var + EPS) * lg_ref[...] + lb_ref[...]
        act_ref[r0:r0 + CONV_ROWS, :] = (y * _sigmoid(y)).astype(BF16)
    o_ref[...] = jnp.dot(act_ref[...], wc_ref[...], preferred_element_type=F32).astype(BF16)


def _conv(layer, v, conv_dw, conv_b, ln_g, ln_b, w_conv_out):
    return pl.pallas_call(
        _conv_kernel,
        grid=(NB,),
        in_specs=[pl.BlockSpec((TB, CC), lambda i: (jnp.maximum(i - 1, 0), 0)),
                  pl.BlockSpec((TB, CC), lambda i: (i, 0)),
                  pl.BlockSpec((TB, CC), lambda i: (jnp.minimum(i + 1, NB - 1), 0)),
                  _const_spec((TAPS, CC), layer), _const_spec((1, CC), layer), _const_spec((1, CC), layer),
                  _const_spec((1, CC), layer), _const_spec((CC, D), layer)],
        out_specs=pl.BlockSpec((TB, D), lambda i: (i, 0)),
        out_shape=jax.ShapeDtypeStruct((N, D), BF16),
        scratch_shapes=[pltpu.VMEM((TB + 2 * HALO, CC), F32), pltpu.VMEM((TB, CC), BF16)],
        compiler_params=_cparams(), name="conv_branch",
    )(v, v, v, conv_dw, conv_b, ln_g, ln_b, w_conv_out)


def _ssm_params(lam_re, lam_im, log_dt, b_re, b_im, c_re, c_im, d):
    dt = jnp.exp(log_dt)[..., None]
    ldr, ldi = lam_re * dt, lam_im * dt
    j = jnp.arange(CH + 1, dtype=F32)
    mag = jnp.exp(ldr[..., None] * j)
    pr, pi = mag * jnp.cos(ldi[..., None] * j), mag * jnp.sin(ldi[..., None] * j)
    nr, ni = pr[..., 1] - 1.0, pi[..., 1]
    den = lam_re * lam_re + lam_im * lam_im
    qr, qi = (nr * lam_re + ni * lam_im) / den, (ni * lam_re - nr * lam_im) / den
    bbr = qr[..., None] * b_re - qi[..., None] * b_im
    bbi = qr[..., None] * b_im + qi[..., None] * b_re
    xr = pr[..., :CH, None] * bbr[..., None, :] - pi[..., :CH, None] * bbi[..., None, :]
    xi = pr[..., :CH, None] * bbi[..., None, :] + pi[..., :CH, None] * bbr[..., None, :]
    kk = (jnp.einsum("dghp,dgpji->dgjhi", c_re, xr, precision=HI)
          - jnp.einsum("dghp,dgpji->dgjhi", c_im, xi, precision=HI))
    kf = jnp.transpose(kk[0], (0, 3, 1, 2))
    kb = jnp.transpose(kk[1], (0, 3, 1, 2))
    eye_h = np.eye(SH, dtype=np.float32)
    centre = kf[:, :, 0:1, :] + kb[:, :, 0:1, :] + d.reshape(SG, 1, 1, SH) * eye_h[None, :, None, :]
    lags = jnp.concatenate([kb[:, :, :0:-1, :], centre, kf[:, :, 1:, :], jnp.zeros((SG, SH, 1, SH), F32)], axis=2)
    m = lags.reshape(SG, SH, 2 * CH * SH)
    wfr = jnp.transpose(xr[0][:, :, ::-1, :], (0, 2, 3, 1)).reshape(SG, CH * SH, SP)
    wfi = jnp.transpose(xi[0][:, :, ::-1, :], (0, 2, 3, 1)).reshape(SG, CH * SH, SP)
    wbr = jnp.transpose(xr[1], (0, 2, 3, 1)).reshape(SG, CH * SH, SP)
    wbi = jnp.transpose(xi[1], (0, 2, 3, 1)).reshape(SG, CH * SH, SP)
    ctr, cti = jnp.transpose(c_re, (0, 1, 3, 2)), jnp.transpose(c_im, (0, 1, 3, 2))
    pfr, pfi = pr[0][..., 1:], pi[0][..., 1:]
    pbr, pbi = pr[1][..., 1:][..., ::-1], pi[1][..., 1:][..., ::-1]
    afr = ctr[0][:, :, None, :] * pfr[..., None] - cti[0][:, :, None, :] * pfi[..., None]
    afi = ctr[0][:, :, None, :] * pfi[..., None] + cti[0][:, :, None, :] * pfr[..., None]
    abr = ctr[1][:, :, None, :] * pbr[..., None] - cti[1][:, :, None, :] * pbi[..., None]
    abi = ctr[1][:, :, None, :] * pbi[..., None] + cti[1][:, :, None, :] * pbr[..., None]
    ws = jnp.concatenate([w.astype(BF16) for w in (wfr, wfi, wbr, wbi)], axis=2)
    vs = jnp.concatenate([a.reshape(SG, SP, CH * SH).astype(BF16) for a in (afr, -afi, abr, -abi)], axis=1)
    a16 = jnp.stack([pr[0][..., CH], pi[0][..., CH], pr[1][..., CH], pi[1][..., CH]], 0)
    a16 = a16.reshape(4, NPAIR, 1, LANES)
    return m, ws, vs, a16


TOE_GROUPS = 8


def _toeplitz_kernel(c_ref, o_ref):
    for g in range(TOE_GROUPS):
        cm = c_ref[g]
        for k in range(CH):
            off = (CH - 1 - k) * SH
            o_ref[g, k * SH:(k + 1) * SH, :] = cm[:, off:off + CH * SH].astype(BF16)


def _toeplitz(lags):
    ng = lags.shape[0]
    return pl.pallas_call(
        _toeplitz_kernel,
        grid=(ng // TOE_GROUPS,),
        in_specs=[pl.BlockSpec((TOE_GROUPS, SH, 2 * CH * SH), lambda i: (i, 0, 0))],
        out_specs=pl.BlockSpec((TOE_GROUPS, CH * SH, CH * SH), lambda i: (i, 0, 0)),
        out_shape=jax.ShapeDtypeStruct((ng, CH * SH, CH * SH), BF16),
        compiler_params=_cparams(), name="s5_toeplitz",
    )(lags)


def _ssm_kernel(u_ref, m_ref, ws_ref, vs_ref, a_ref, h0_ref, y_ref, fin_ref, s_ref, ef_ref):
    u0, u1 = u_ref[0], u_ref[1]
    s0 = jnp.dot(u0, ws_ref[0], preferred_element_type=F32)
    s1 = jnp.dot(u1, ws_ref[1], preferred_element_type=F32)
    for k in range(4):
        s_ref[k] = jnp.concatenate([s0[:, k * SP:(k + 1) * SP], s1[:, k * SP:(k + 1) * SP]], axis=1)

    def scan(kr, ki, first, nrows, nsteps, reverse, er, ei):
        ar, ai = a_ref[kr], a_ref[ki]
        order = range(nsteps - 1, -1, -1) if reverse else range(nsteps)
        for c in order:
            rows = pl.ds(first + c, nrows, stride=nsteps)
            ef_ref[kr, rows, :] = er
            ef_ref[ki, rows, :] = ei
            sr, si = s_ref[kr, rows, :], s_ref[ki, rows, :]
            er, ei = ar * er - ai * ei + sr, ar * ei + ai * er + si
        return er, ei

    zc = jnp.zeros((BATCH, LANES), F32)
    for kr, ki, rev in ((0, 1, False), (2, 3, True)):
        er, ei = scan(kr, ki, 0, BATCH, CPS_CTX, rev, zc, zc)
        fin_ref[kr] = er
        fin_ref[ki] = ei
        scan(kr, ki, NCH_CTX, DEC_BATCH, CPS_LAT, rev, h0_ref[kr], h0_ref[ki])

    for q, uq in ((0, u0), (1, u1)):
        ef = jnp.concatenate([ef_ref[k, :, q * SP:(q + 1) * SP] for k in range(4)], axis=1).astype(BF16)
        y_ref[q] = (jnp.dot(uq, m_ref[q], preferred_element_type=F32)
                    + jnp.dot(ef, vs_ref[q], preferred_element_type=F32))


def _ssm(layer, u_chunks, m, ws, vs, a16, h0):
    pair = lambda j: (j, 0, 0)
    return pl.pallas_call(
        _ssm_kernel,
        grid=(NPAIR,),
        in_specs=[pl.BlockSpec((2, NCH, CH * SH), pair),
                  pl.BlockSpec((None, 2, CH * SH, CH * SH), lambda j: (layer, j, 0, 0)),
                  pl.BlockSpec((None, 2, CH * SH, 4 * SP), lambda j: (layer, j, 0, 0)),
                  pl.BlockSpec((None, 2, 4 * SP, CH * SH), lambda j: (layer, j, 0, 0)),
                  pl.BlockSpec((None, 4, None, 1, LANES), lambda j: (layer, 0, j, 0, 0)),
                  pl.BlockSpec((None, 4, None, DEC_BATCH, LANES), lambda j: (layer, 0, j, 0, 0))],
        out_specs=[pl.BlockSpec((2, NCH, CH * SH), pair),
                   pl.BlockSpec((4, None, BATCH, LANES), lambda j: (0, j, 0, 0))],
        out_shape=[jax.ShapeDtypeStruct((SG, NCH, CH * SH), F32),
                   jax.ShapeDtypeStruct((4, NPAIR, BATCH, LANES), F32)],
        scratch_shapes=[pltpu.VMEM((4, NCH, LANES), F32), pltpu.VMEM((4, NCH, LANES), F32)],
        compiler_params=_cparams(), name="s5_chunked",
    )(u_chunks, m, ws, vs, a16, h0)


def _merge_kernel(x_ref, bfc_ref, bfl_ref, bc_ref, ys_ref, gate_ref, mod_ref, wg_ref, bg_ref, wso_ref, wo_ref,
                  n2_ref, rwh_ref, rwl_ref, rb_ref, xo_ref, h_ref, ti_ref, tp_ref, slab_ref):
    i = pl.program_id(0)
    g = jax.nn.gelu(_chunks_to_tokens(ys_ref, slab_ref))
    gl = jnp.dot(g.astype(BF16), wg_ref[...], preferred_element_type=F32) + bg_ref[...]
    y2 = g * _sigmoid(gl)
    brs = jnp.dot(y2.astype(BF16), wso_ref[...], preferred_element_type=F32)
    brf = jnp.where(i < CTX_BLOCKS, bfc_ref[...], bfl_ref[...]).astype(F32)
    mixed = (gate_ref[:, 0:D].astype(F32) * brf
             + gate_ref[:, D:2 * D].astype(F32) * bc_ref[...].astype(F32)
             + gate_ref[:, 2 * D:3 * D].astype(F32) * brs)
    g1 = mod_ref[:, 2 * D:3 * D]
    x = x_ref[...] + g1 * jnp.dot(mixed.astype(BF16), wo_ref[...], preferred_element_type=F32)
    xo_ref[...] = x
    sh2 = mod_ref[:, 3 * D:4 * D]
    sc2 = mod_ref[:, 4 * D:5 * D]
    ms = jnp.mean(x * x, axis=-1, keepdims=True)
    h = (x * lax.rsqrt(ms + EPS)) * n2_ref[...] * (1.0 + sc2) + sh2
    _std_to_rows(h_ref, h, TB)
    hh = h.astype(BF16)
    hl = (h - hh.astype(F32)).astype(BF16)
    logits = (jnp.dot(hh, rwh_ref[...], preferred_element_type=F32)
              + jnp.dot(hl, rwh_ref[...], preferred_element_type=F32)
              + jnp.dot(hh, rwl_ref[...], preferred_element_type=F32)) + rb_ref[...]
    lane = lax.broadcasted_iota(jnp.int32, (TB, LANES), 1).astype(F32)
    vals, idxs = [], []
    l = logits
    for _ in range(TOPK):
        m = jnp.max(l, axis=-1, keepdims=True)
        idx = jnp.min(jnp.where(l == m, lane, float(LANES)), axis=-1, keepdims=True)
        vals.append(m)
        idxs.append(idx)
        l = jnp.where(lane == idx, -jnp.inf, l)
    es = [jnp.exp(v - vals[0]) for v in vals]
    tot = es[0] + es[1] + es[2] + es[3]
    ti = jnp.zeros((TB, LANES), F32)
    tp = jnp.zeros((TB, LANES), F32)
    for k in range(TOPK):
        ti = jnp.where(lane == float(k), idxs[k], ti)
        tp = jnp.where(lane == float(k), es[k] / tot, tp)
    ti_ref[...] = ti.astype(jnp.int32)
    tp_ref[...] = tp


def _merge(layer, x, brf_ctx, brf_lat, brc, ys, gates, mod, w_glu, b_glu, w_ssm_out, w_out, norm2_g,
           rw_hi, rw_lo, rb):
    row = lambda i: (i, 0)
    return pl.pallas_call(
        _merge_kernel,
        grid=(NB,),
        in_specs=[pl.BlockSpec((TB, D), row),
                  pl.BlockSpec((TB, D), lambda i: (jnp.minimum(i, CTX_BLOCKS - 1), 0)),
                  pl.BlockSpec((TB, D), lambda i: (jnp.maximum(i - CTX_BLOCKS, 0), 0)),
                  pl.BlockSpec((TB, D), row),
                  pl.BlockSpec((SG, CPB, CH * SH), lambda i: (0, i, 0)),
                  pl.BlockSpec((TB, 3 * D), row),
                  pl.BlockSpec((None, 1, NMOD * D), lambda i: (_mod_row(i), 0, 0)),
                  _const_spec((SW, SW), layer), _const_spec((1, SW), layer), _const_spec((SW, D), layer),
                  _const_spec((D, D), layer), _const_spec((1, D), layer),
                  _const_spec((D, LANES), layer), _const_spec((D, LANES), layer), _const_spec((1, LANES), layer)],
        out_specs=[pl.BlockSpec((TB, D), row), pl.BlockSpec((TB * DC, LANES), row),
                   pl.BlockSpec((TB, LANES), row), pl.BlockSpec((TB, LANES), row)],
        out_shape=[jax.ShapeDtypeStruct((N, D), F32), jax.ShapeDtypeStruct((N * DC, LANES), F32),
                   jax.ShapeDtypeStruct((N, LANES), jnp.int32), jax.ShapeDtypeStruct((N, LANES), F32)],
        scratch_shapes=[pltpu.VMEM((NM, TB, LANES), F32)],
        compiler_params=_cparams(), name="merge_router",
    )(x, brf_ctx, brf_lat, brc, ys, gates, mod, w_glu, b_glu, w_ssm_out, w_out, norm2_g, rw_hi, rw_lo, rb)


def _routing_tables(top_i, top_p):
    e_flat = top_i[:, :TOPK].reshape(-1)
    p_flat = top_p[:, :TOPK].reshape(-1)
    na = N * TOPK
    key = e_flat * 32768 + jnp.arange(na, dtype=jnp.int32)
    skey, sw = lax.sort_key_val(key, p_flat)
    tok8 = ((skey & 32767) // TOPK) * DC
    tok8 = jnp.concatenate([tok8, jnp.zeros((TM,), jnp.int32)])
    sw = jnp.concatenate([sw, jnp.zeros((TM,), F32)])
    experts = jnp.arange(NE, dtype=jnp.int32)
    counts = jnp.sum((e_flat[:, None] == experts[None, :]).astype(jnp.int32), axis=0)
    ntile = (counts + TM - 1) // TM
    tend = jnp.cumsum(ntile)
    tstart = tend - ntile
    cstart = jnp.cumsum(counts) - counts
    n_tiles = tend[-1]
    later = jnp.logical_and(experts[None, :] > experts[:, None], ntile[None, :] > 0)
    nxt_e = jnp.min(jnp.where(later, experts[None, :], NE), axis=1)
    nxt_e = jnp.where(nxt_e == NE, -1, nxt_e)
    used = (ntile > 0).astype(jnp.int32)
    used_before = jnp.cumsum(used) - used
    ti = jnp.arange(NT_MAX + 1, dtype=jnp.int32)
    t = jnp.minimum(ti, n_tiles - 1)
    tile_e = jnp.minimum(jnp.sum((t[:, None] >= tend[None, :]).astype(jnp.int32), axis=1), NE - 1)
    onehot = (tile_e[:, None] == experts[None, :]).astype(jnp.int32)
    pick = lambda tab: jnp.sum(onehot * tab[None, :], axis=1)
    k = t - pick(tstart)
    tile_j0 = pick(cstart) + k * TM
    tile_cnt = jnp.clip(pick(counts) - k * TM, 0, TM)
    tile_first = jnp.logical_and(k == 0, ti < n_tiles)
    tile_slot = pick(used_before) % 2
    tile_next = pick(nxt_e)
    i32 = lambda a: a.astype(jnp.int32)
    return (tok8, sw, i32(tile_e), i32(tile_j0), i32(tile_cnt), i32(tile_first), i32(tile_slot), i32(tile_next),
            i32(n_tiles.reshape(1)))


GROWS = 16
GTILES = 2
assert NT_MAX % GTILES == 0


def _gather_kernel(tok8_ref, j0_ref, nt_ref, h_ref, o_ref, g_ref):
    for half in range(GTILES):
        t = pl.program_id(0) * GTILES + half
        out_rows = slice(half * TM, (half + 1) * TM)

        @pl.when(t < nt_ref[0])
        def _():
            j0 = j0_ref[t]

            def body(b, carry):
                base = j0 + b * GROWS
                rows = [h_ref[pl.ds(pl.multiple_of(tok8_ref[base + j], DC), DC), :] for j in range(GROWS)]
                for j in range(GROWS):
                    g_ref[pl.ds(pl.multiple_of((b * GROWS + j) * DC, DC), DC), :] = rows[j]
                return carry

            lax.fori_loop(0, TM // GROWS, body, 0)
            for c in range(DC):
                o_ref[out_rows, c * LANES:(c + 1) * LANES] = g_ref[pl.ds(c, TM, stride=DC), :].astype(BF16)

        @pl.when(t >= nt_ref[0])
        def _():
            o_ref[out_rows, :] = jnp.zeros((TM, D), BF16)


def _gather(h_rows, tok8, tile_j0, n_tiles):
    return pl.pallas_call(
        _gather_kernel,
        grid_spec=pltpu.PrefetchScalarGridSpec(
            num_scalar_prefetch=3, grid=(NT_MAX // GTILES,),
            in_specs=[pl.BlockSpec((N * DC, LANES), lambda i, *_: (0, 0))],
            out_specs=pl.BlockSpec((GTILES * TM, D), lambda i, *_: (i, 0)),
            scratch_shapes=[pltpu.VMEM((TM * DC, LANES), F32)]),
        out_shape=jax.ShapeDtypeStruct((R_MAX, D), BF16),
        compiler_params=_cparams(), name="moe_gather",
    )(tok8, tile_j0, n_tiles, h_rows)


def _experts_kernel(layer, tok8_ref, w_ref, te_ref, j0_ref, cnt_ref, first_ref, slot_ref, next_ref, nt_ref,
                    x_ref, wgu_hbm, bgu_ref, wdn_hbm, bdn_ref, acc_ref, y_ref, wgu_buf, wdn_buf, sem):
    i = pl.program_id(0)

    def weight_copies(e, slot):
        return (pltpu.make_async_copy(wgu_hbm.at[layer, e], wgu_buf.at[slot], sem.at[0, slot]),
                pltpu.make_async_copy(wdn_hbm.at[layer, e], wdn_buf.at[slot], sem.at[1, slot]))

    @pl.when(i == 0)
    def _():
        for cp in weight_copies(te_ref[0], 0):
            cp.start()
        acc_ref[...] = jnp.zeros_like(acc_ref)
        y_ref[...] = jnp.zeros_like(y_ref)

    live = i < nt_ref[0]
    slot = slot_ref[i]

    def scatter_previous_tile():
        p = jnp.maximum(i - 1, 0)
        j0 = j0_ref[p]
        cnt = jnp.where(i > 0, cnt_ref[p], 0)
        ys = y_ref.at[(i + 1) % 2]
        for r0 in range(0, TM, SCAT):
            offs, ws = [], []
            for j in range(SCAT):
                real = r0 + j < cnt
                offs.append(pl.multiple_of(jnp.where(real, tok8_ref[j0 + r0 + j], N * DC), DC))
                ws.append(jnp.where(real, w_ref[j0 + r0 + j], 0.0))
            olds = [acc_ref[pl.ds(o, DC), :] for o in offs]
            news = [olds[j] + ws[j] * ys[(r0 + j) * DC:(r0 + j + 1) * DC, :] for j in range(SCAT)]
            for j in range(SCAT):
                acc_ref[pl.ds(offs[j], DC), :] = news[j]

    @pl.when(jnp.logical_and(live, first_ref[i] == 1))
    def _():
        for cp in weight_copies(te_ref[i], slot):
            cp.wait()

        @pl.when(next_ref[i] >= 0)
        def _():
            for cp in weight_copies(next_ref[i], 1 - slot):
                cp.start()

    @pl.when(live)
    def _():
        scatter_previous_tile()
        x = x_ref[...]
        y = jnp.broadcast_to(bdn_ref[...], (TM, D))
        for f in range(0, DFF, FC):
            wg = wgu_buf[slot, :, f:f + FC].astype(BF16)
            wu = wgu_buf[slot, :, DFF + f:DFF + f + FC].astype(BF16)
            gate = jnp.dot(x, wg, preferred_element_type=F32) + bgu_ref[:, f:f + FC]
            up = jnp.dot(x, wu, preferred_element_type=F32) + bgu_ref[:, DFF + f:DFF + f + FC]
            gate = jnp.minimum(gate, LIMIT)
            up = jnp.clip(up, -LIMIT, LIMIT)
            act = (up + 1.0) * (gate * _sigmoid(ALPHA * gate))
            y = y + jnp.dot(act.astype(BF16), wdn_buf[slot, f:f + FC, :].astype(BF16), preferred_element_type=F32)
        _std_to_rows(y_ref.at[i % 2], y, TM)

    @pl.when(i == nt_ref[0])
    def _():
        scatter_previous_tile()


def _experts(layer, x_sorted, tables, w_gate_up, b_gate_up, w_down, b_down):
    tile = lambda i, *s: (jnp.minimum(i, s[-1][0] - 1), 0)
    bias = lambda i, *s: (layer, s[2][i], 0, 0)
    return pl.pallas_call(
        functools.partial(_experts_kernel, layer),
        grid_spec=pltpu.PrefetchScalarGridSpec(
            num_scalar_prefetch=len(tables), grid=(NT_MAX + 1,),
            in_specs=[pl.BlockSpec((TM, D), tile),
                      pl.BlockSpec(memory_space=pl.ANY),
                      pl.BlockSpec((None, None, 1, 2 * DFF), bias),
                      pl.BlockSpec(memory_space=pl.ANY),
                      pl.BlockSpec((None, None, 1, D), bias)],
            out_specs=pl.BlockSpec(((N + SUB) * DC, LANES), lambda i, *_: (0, 0)),
            scratch_shapes=[pltpu.VMEM((2, TM * DC, LANES), F32),
                            pltpu.VMEM((2, D, 2 * DFF), F32), pltpu.VMEM((2, DFF, D), F32),
                            pltpu.SemaphoreType.DMA((2, 2))]),
        out_shape=jax.ShapeDtypeStruct(((N + SUB) * DC, LANES), F32),
        compiler_params=_cparams(VMEM_MOE), name="moe_experts",
    )(*tables, x_sorted, w_gate_up, b_gate_up.reshape(DEPTH, NE, 1, 2 * DFF), w_down,
      b_down.reshape(DEPTH, NE, 1, D))


def _final_kernel(x_ref, moe_ref, mod_ref, g_ref, octx_ref, olat_ref):
    i = pl.program_id(0)
    x = x_ref[...] + mod_ref[:, 5 * D:6 * D] * _rows_to_std(moe_ref, TB)
    ms = jnp.mean(x * x, axis=-1, keepdims=True)
    y = (x * lax.rsqrt(ms + EPS)) * g_ref[...]

    @pl.when(i < CTX_BLOCKS)
    def _():
        octx_ref[...] = y

    @pl.when(i >= CTX_BLOCKS)
    def _():
        olat_ref[...] = y


def _final(x, moe, mod, g):
    row = lambda i: (i, 0)
    return pl.pallas_call(
        _final_kernel,
        grid=(NB,),
        in_specs=[pl.BlockSpec((TB, D), row), pl.BlockSpec((TB * DC, LANES), row),
                  pl.BlockSpec((None, 1, NMOD * D), lambda i: (_mod_row(i), 0, 0)), _const_spec((1, D))],
        out_specs=[pl.BlockSpec((TB, D), lambda i: (jnp.minimum(i, CTX_BLOCKS - 1), 0)),
                   pl.BlockSpec((TB, D), lambda i: (jnp.maximum(i - CTX_BLOCKS, 0), 0))],
        out_shape=[jax.ShapeDtypeStruct((N_CTX, D), F32), jax.ShapeDtypeStruct((N_LAT, D), F32)],
        compiler_params=_cparams(), name="final_norm",
    )(x, moe, mod, g)


def _grid_pos_embed(rows):
    quarter = D // 4
    freqs = jnp.exp(-math.log(10000.0) * jnp.arange(quarter, dtype=F32) / quarter)
    r = jnp.repeat(jnp.arange(rows, dtype=F32), GRID_W)
    col = jnp.tile(jnp.arange(GRID_W, dtype=F32), rows)
    ar = r[:, None] * freqs
    ac = col[:, None] * freqs
    return jnp.concatenate([jnp.sin(ar), jnp.cos(ar), jnp.sin(ac), jnp.cos(ac)], axis=-1)


def kernel(x_prompt, x_sample, state_ssm_re, state_ssm_im, c, c_ctx, w_mod, b_mod, norm1_g, norm2_g, w_in, b_in, w_four, conv_dw, conv_dw_b, conv_ln_g, conv_ln_b, w_conv_out, ssm_lam_re, ssm_lam_im, ssm_log_dt, ssm_b_re, ssm_b_im, ssm_c_re, ssm_c_im, ssm_d, w_ssm_glu, b_ssm_glu, w_ssm_out, w_out, router_w, router_b, w_gate_up, b_gate_up, w_down, b_down, final_norm_g):
    x = (x_prompt.reshape(N_CTX, D), x_sample.reshape(N_LAT, D), _grid_pos_embed(DEC_SEQ // GRID_W))

    cond8 = jnp.concatenate([c_ctx[None, :], c, jnp.zeros((SUB - 1 - DEC_BATCH, D), F32)], axis=0)
    mod_all = _adaln(cond8, w_mod, b_mod)[:, :1 + DEC_BATCH].reshape(DEPTH, 1 + DEC_BATCH, 1, NMOD * D)

    w_in_b, w_four_b, w_conv_b = w_in.astype(BF16), w_four.astype(BF16), w_conv_out.astype(BF16)
    w_glu_b, w_sso_b, w_out_b = w_ssm_glu.astype(BF16), w_ssm_out.astype(BF16), w_out.astype(BF16)
    rw = jnp.pad(router_w, ((0, 0), (0, 0), (0, LANES - NE)))
    rw_hi = rw.astype(BF16)
    rw_lo = (rw - rw_hi.astype(F32)).astype(BF16)
    rb = jnp.pad(router_b, ((0, 0), (0, LANES - NE)), constant_values=-1e30).reshape(DEPTH, 1, LANES)
    r3 = lambda a: a.reshape(DEPTH, 1, a.shape[-1])

    lags_all, ws_all, vs_all, a16_all = jax.vmap(_ssm_params)(
        ssm_lam_re, ssm_lam_im, ssm_log_dt, ssm_b_re, ssm_b_im, ssm_c_re, ssm_c_im, ssm_d)
    m_all = _toeplitz(lags_all.reshape(DEPTH * SG, SH, 2 * CH * SH)).reshape(DEPTH, SG, CH * SH, CH * SH)
    h0_all = jnp.stack([state_ssm_re[:, :, 0], state_ssm_im[:, :, 0], state_ssm_re[:, :, 1], state_ssm_im[:, :, 1]],
                       axis=0)
    h0_all = h0_all.reshape(4, DEC_BATCH, DEPTH, NPAIR, LANES).transpose(2, 0, 3, 1, 4)

    fins = []
    moe = None
    mod_prev = None
    for l in range(DEPTH):
        mod = mod_all[l]
        x, zf, v, u, gates = _inproj(l, x, moe, mod_prev, mod, r3(norm1_g), w_in_b, r3(b_in))
        brf_ctx = _fourier(l, zf, w_four_b, SEQ, 0, BATCH)
        brf_lat = _fourier(l, zf, w_four_b, DEC_SEQ, N_CTX // DEC_SEQ, DEC_BATCH)
        brc = _conv(l, v, conv_dw, r3(conv_dw_b), r3(conv_ln_g), r3(conv_ln_b), w_conv_b)
        ys, fin = _ssm(l, u, m_all, ws_all, vs_all, a16_all, h0_all)
        fins.append(fin)

        x, h_rows, top_i, top_p = _merge(l, x, brf_ctx, brf_lat, brc, ys, gates, mod, w_glu_b, r3(b_ssm_glu),
                                         w_sso_b, w_out_b, r3(norm2_g), rw_hi, rw_lo, rb)
        tables = _routing_tables(top_i, top_p)
        x_sorted = _gather(h_rows, tables[0], tables[3], tables[-1])
        moe = _experts(l, x_sorted, tables, w_gate_up, b_gate_up, w_down, b_down)
        mod_prev = mod

    y_ctx, y_lat = _final(x, moe, mod_prev, final_norm_g.reshape(1, D))
    fin = jnp.stack(fins, axis=0).transpose(3, 0, 1, 2, 4).reshape(BATCH, DEPTH, 2, 2, SG, SP)
    new_re = fin[:, :, :, 0]
    new_im = fin[:, :, :, 1]
    return (y_ctx.reshape(BATCH, SEQ, D), y_lat.reshape(DEC_BATCH, DEC_SEQ, D), new_re, new_im)
```

```python
import functools
import math

import numpy as np
import jax
import jax.numpy as jnp
from jax import lax
from jax.experimental import pallas as pl
from jax.experimental.pallas import tpu as pltpu

F32 = jnp.float32
BF16 = jnp.bfloat16
HI = lax.Precision.HIGHEST

D = 1024
BATCH, SEQ = 16, 256
DEC_BATCH, DEC_SEQ = 2, 1024
DEPTH = 4
N_CTX = BATCH * SEQ
N_LAT = DEC_BATCH * DEC_SEQ
N = N_CTX + N_LAT
GRID_W = 64
FW, FG = 512, 4
FGD = FW // FG
CC, TAPS = 512, 31
PAD = TAPS // 2
SW, SH = 512, 16
SG = SW // SH
SP = 64
IN_COLS = FW + 2 * CC + SW + 3 * D
NE, TOPK, DFF = 32, 4, 1024
LIMIT, ALPHA = 7.0, 1.702
EPS = 1e-6
NMOD = 6

TB = 256
NB = N // TB
CTX_BLOCKS = N_CTX // TB
LAT_BLOCKS_PER_SEQ = DEC_SEQ // TB
CH = 16
NCH = N // CH
NCH_CTX = N_CTX // CH
CPS_CTX = SEQ // CH
CPS_LAT = DEC_SEQ // CH
NPAIR = SG // 2
TM = 256
NT_MAX = (N * TOPK + NE * (TM - 1) + TM - 1) // TM
R_MAX = NT_MAX * TM
SUB = 8
LANES = 128
DC = D // LANES
SCAT = 16
FC = 1024
VMEM_DEFAULT = 48 * 1024 * 1024
VMEM_MOE = 62 * 1024 * 1024


def _cparams(vmem=VMEM_DEFAULT, ndim=1):
    return pltpu.CompilerParams(dimension_semantics=("arbitrary",) * ndim, vmem_limit_bytes=vmem)


def _const_spec(shape, layer=None):
    if layer is None:
        zeros = (0,) * len(shape)
        return pl.BlockSpec(shape, lambda *_: zeros)
    zeros = (0,) * len(shape)
    return pl.BlockSpec((None,) + tuple(shape), lambda *_: (layer,) + zeros, pipeline_mode=pl.Buffered(1))


def _mod_row(i):
    return jnp.where(i < CTX_BLOCKS, 0, 1 + (i - CTX_BLOCKS) // LAT_BLOCKS_PER_SEQ)


def _sigmoid(x):
    return 1.0 / (1.0 + jnp.exp(-x))


def _rows_to_std(ref, rows):
    return jnp.concatenate([ref[pl.ds(c, rows, stride=DC), :] for c in range(DC)], axis=1)


def _std_to_rows(ref, val, rows):
    for c in range(DC):
        ref[pl.ds(c, rows, stride=DC), :] = val[:, c * LANES:(c + 1) * LANES]


NM = SW // LANES
GPV = LANES // SH
CPB = TB // CH


def _lane_group():
    return lax.broadcasted_iota(jnp.int32, (CPB, LANES), 1) // SH


def _tokens_to_chunks(u, slab_ref, out_ref):
    for m in range(NM):
        slab_ref[m] = u[:, m * LANES:(m + 1) * LANES]
    grp = _lane_group()
    for m in range(NM):
        for kk in range(CH // GPV):
            zs = [slab_ref[m, pl.ds(kk * GPV + k8, CPB, stride=CH), :] for k8 in range(GPV)]
            for r in range(GPV):
                acc = None
                for k8 in range(GPV):
                    sh = ((k8 - r) % GPV) * SH
                    piece = zs[k8] if sh == 0 else pltpu.roll(zs[k8], sh, 1)
                    acc = piece if acc is None else jnp.where(grp == k8, piece, acc)
                out_ref[m * GPV + r, :, kk * LANES:(kk + 1) * LANES] = acc.astype(out_ref.dtype)


def _chunks_to_tokens(y_ref, slab_ref):
    grp = _lane_group()
    for m in range(NM):
        for kk in range(CH // GPV):
            ys = [y_ref[m * GPV + r, :, kk * LANES:(kk + 1) * LANES] for r in range(GPV)]
            for t8 in range(GPV):
                acc = None
                for r in range(GPV):
                    sh = ((r - t8) % GPV) * SH
                    piece = ys[r] if sh == 0 else pltpu.roll(ys[r], sh, 1)
                    acc = piece if acc is None else jnp.where(grp == r, piece, acc)
                slab_ref[m, pl.ds(kk * GPV + t8, CPB, stride=CH), :] = acc
    return jnp.concatenate([slab_ref[m] for m in range(NM)], axis=1)


MOD_COLS = 1536


def _mod_kernel(cond_ref, w_ref, b_ref, o_ref):
    c = cond_ref[...]
    s = c * _sigmoid(c)
    o_ref[...] = jnp.dot(s.astype(BF16), w_ref[...].astype(BF16), preferred_element_type=F32) + b_ref[...]


def _adaln(cond8, w_mod, b_mod):
    nc = NMOD * D // MOD_COLS
    return pl.pallas_call(
        _mod_kernel,
        grid=(DEPTH, nc),
        in_specs=[
            pl.BlockSpec((SUB, D), lambda l, j: (0, 0)),
            pl.BlockSpec((None, D, MOD_COLS), lambda l, j: (l, 0, j)),
            pl.BlockSpec((None, 1, MOD_COLS), lambda l, j: (l, 0, j)),
        ],
        out_specs=pl.BlockSpec((None, SUB, MOD_COLS), lambda l, j: (l, 0, j)),
        out_shape=jax.ShapeDtypeStruct((DEPTH, SUB, NMOD * D), F32),
        compiler_params=_cparams(ndim=2),
        name="adaln_mod",
    )(cond8, w_mod, b_mod.reshape(DEPTH, 1, NMOD * D))


def _inproj_kernel(has_moe, *refs):
    if has_moe:
        (x_ref, moe_ref, modp_ref, mod_ref, g_ref, w_ref, b_ref,
         xo_ref, zf_ref, v_ref, u_ref, gate_ref, slab_ref) = refs
        x = x_ref[...] + modp_ref[:, 5 * D:6 * D] * _rows_to_std(moe_ref, TB)
        xo_ref[...] = x
    else:
        (xp_ref, xs_ref, pos_ref, mod_ref, g_ref, w_ref, b_ref,
         xo_ref, zf_ref, v_ref, u_ref, gate_ref, slab_ref) = refs
        x = jnp.where(pl.program_id(0) < CTX_BLOCKS, xp_ref[...], xs_ref[...] + pos_ref[...])
        xo_ref[...] = x
    sh1 = mod_ref[:, 0:D]
    sc1 = mod_ref[:, D:2 * D]
    ms = jnp.mean(x * x, axis=-1, keepdims=True)
    h = (x * lax.rsqrt(ms + EPS)) * g_ref[...] * (1.0 + sc1) + sh1
    z = jnp.dot(h.astype(BF16), w_ref[...], preferred_element_type=F32) + b_ref[...]
    o = 0
    zf_ref[...] = z[:, o:o + FW].astype(BF16)
    o += FW
    za = z[:, o:o + CC]
    zb = z[:, o + CC:o + 2 * CC]
    v_ref[...] = (za * _sigmoid(zb)).astype(BF16)
    o += 2 * CC
    _tokens_to_chunks(z[:, o:o + SW], slab_ref, u_ref)
    o += SW
    gate_ref[...] = _sigmoid(z[:, o:]).astype(BF16)


def _inproj(layer, x, moe, mod_prev, mod, norm_g, w_in, b_in):
    has_moe = moe is not None
    row = lambda i: (i, 0)
    mod_spec = pl.BlockSpec((None, 1, NMOD * D), lambda i: (_mod_row(i), 0, 0))
    if has_moe:
        in_specs = [pl.BlockSpec((TB, D), row), pl.BlockSpec((TB * DC, LANES), row), mod_spec]
        args = [x, moe, mod_prev]
    else:
        lat = lambda i: jnp.maximum(i - CTX_BLOCKS, 0)
        in_specs = [pl.BlockSpec((TB, D), lambda i: (jnp.minimum(i, CTX_BLOCKS - 1), 0)),
                    pl.BlockSpec((TB, D), lambda i: (lat(i), 0)),
                    pl.BlockSpec((TB, D), lambda i: (lat(i) % LAT_BLOCKS_PER_SEQ, 0))]
        args = list(x)
    in_specs += [mod_spec, _const_spec((1, D), layer), _const_spec((D, IN_COLS), layer),
                 _const_spec((1, IN_COLS), layer)]
    args += [mod, norm_g, w_in, b_in]
    out_specs = [pl.BlockSpec((TB, FW), row), pl.BlockSpec((TB, CC), row),
                 pl.BlockSpec((SG, CPB, CH * SH), lambda i: (0, i, 0)), pl.BlockSpec((TB, 3 * D), row)]
    out_shape = [jax.ShapeDtypeStruct((N, FW), BF16), jax.ShapeDtypeStruct((N, CC), BF16),
                 jax.ShapeDtypeStruct((SG, NCH, CH * SH), BF16), jax.ShapeDtypeStruct((N, 3 * D), BF16)]
    out_specs = [pl.BlockSpec((TB, D), row)] + out_specs
    out_shape = [jax.ShapeDtypeStruct((N, D), F32)] + out_shape
    return pl.pallas_call(
        functools.partial(_inproj_kernel, has_moe),
        grid=(NB,), in_specs=in_specs, out_specs=out_specs, out_shape=out_shape,
        scratch_shapes=[pltpu.VMEM((NM, TB, LANES), F32)],
        compiler_params=_cparams(), name=f"inproj_{int(has_moe)}",
    )(*args)


def _fourier_kernel(scale, zf_ref, csbd_ref, csl_ref, wf_ref, o_ref):
    t = jnp.dot(zf_ref[...], csbd_ref[...], preferred_element_type=F32).astype(BF16)
    ts = jnp.concatenate([t[:, :FW], t[:, FW:]], axis=0)
    y = jnp.dot(csl_ref[...], ts, preferred_element_type=F32) * scale
    o_ref[...] = jnp.dot(y.astype(BF16), wf_ref[...], preferred_element_type=F32).astype(BF16)


def _dft_consts(L):
    k = np.arange(L)
    ang = 2.0 * np.pi * ((k[:, None] * k[None, :]) % L) / L
    csl = np.concatenate([np.cos(ang), -np.sin(ang)], axis=1)
    m = np.arange(FGD)
    angc = 2.0 * np.pi * ((m[:, None] * m[None, :]) % FGD) / FGD
    cbd = np.kron(np.eye(FG), np.cos(angc))
    sbd = np.kron(np.eye(FG), np.sin(angc))
    csbd = np.concatenate([cbd, sbd], axis=1)
    return jnp.asarray(csl, F32).astype(BF16), jnp.asarray(csbd, F32).astype(BF16)


def _fourier(layer, zf, w_four, L, first_block, nseq):
    csl, csbd = _dft_consts(L)
    scale = 1.0 / math.sqrt(L * FGD)
    return pl.pallas_call(
        functools.partial(_fourier_kernel, scale),
        grid=(nseq,),
        in_specs=[pl.BlockSpec((L, FW), lambda i: (i + first_block, 0)),
                  _const_spec((FW, 2 * FW)), _const_spec((L, 2 * L)), _const_spec((FW, D), layer)],
        out_specs=pl.BlockSpec((L, D), lambda i: (i, 0)),
        out_shape=jax.ShapeDtypeStruct((nseq * L, D), BF16),
        compiler_params=_cparams(), name=f"fourier_{L}",
    )(zf, csbd, csl, w_four)


HALO = 16
CONV_ROWS = 32


def _conv_kernel(vp_ref, vc_ref, vn_ref, cw_ref, cb_ref, lg_ref, lb_ref, wc_ref, o_ref, pad_ref, act_ref):
    i = pl.program_id(0)
    pos = (i - CTX_BLOCKS) % LAT_BLOCKS_PER_SEQ
    lat = i >= CTX_BLOCKS
    has_prev = jnp.logical_and(lat, pos != 0)
    has_next = jnp.logical_and(lat, pos != LAT_BLOCKS_PER_SEQ - 1)
    prev = vp_ref[TB - HALO:, :].astype(F32)
    nxt = vn_ref[:HALO, :].astype(F32)
    pad_ref[0:HALO, :] = jnp.where(has_prev, prev, 0.0)
    pad_ref[HALO:HALO + TB, :] = vc_ref[...].astype(F32)
    pad_ref[HALO + TB:, :] = jnp.where(has_next, nxt, 0.0)
    for r0 in range(0, TB, CONV_ROWS):
        acc = jnp.broadcast_to(cb_ref[...], (CONV_ROWS, CC))
        for s in range(SUB):
            part = None
            for q in range((TAPS + SUB) // SUB):
                o = SUB * q + s - (HALO - PAD)
                if 0 <= o < TAPS:
                    term = pad_ref[r0 + SUB * q:r0 + SUB * q + CONV_ROWS + SUB, :] * cw_ref[o:o + 1, :]
                    part = term if part is None else part + term
            acc = acc + part[s:s + CONV_ROWS, :]
        mu = jnp.mean(acc, axis=-1, keepdims=True)
        xc = acc - mu
        var = jnp.mean(xc * xc, axis=-1, keepdims=True)
        y = xc * lax.rsqrt(var + EPS) * lg_ref[...] + lb_ref[...]
        act_ref[r0:r0 + CONV_ROWS, :] = (y * _sigmoid(y)).astype(BF16)
    o_ref[...] = jnp.dot(act_ref[...], wc_ref[...], preferred_element_type=F32).astype(BF16)


def _conv(layer, v, conv_dw, conv_b, ln_g, ln_b, w_conv_out):
    return pl.pallas_call(
        _conv_kernel,
        grid=(NB,),
        in_specs=[pl.BlockSpec((TB, CC), lambda i: (jnp.maximum(i - 1, 0), 0)),
                  pl.BlockSpec((TB, CC), lambda i: (i, 0)),
                  pl.BlockSpec((TB, CC), lambda i: (jnp.minimum(i + 1, NB - 1), 0)),
                  _const_spec((TAPS, CC), layer), _const_spec((1, CC), layer), _const_spec((1, CC), layer),
                  _const_spec((1, CC), layer), _const_spec((CC, D), layer)],
        out_specs=pl.BlockSpec((TB, D), lambda i: (i, 0)),
        out_shape=jax.ShapeDtypeStruct((N, D), BF16),
        scratch_shapes=[pltpu.VMEM((TB + 2 * HALO, CC), F32), pltpu.VMEM((TB, CC), BF16)],
        compiler_params=_cparams(), name="conv_branch",
    )(v, v, v, conv_dw, conv_b, ln_g, ln_b, w_conv_out)


def _ssm_params(lam_re, lam_im, log_dt, b_re, b_im, c_re, c_im, d):
    dt = jnp.exp(log_dt)[..., None]
    ldr, ldi = lam_re * dt, lam_im * dt
    j = jnp.arange(CH + 1, dtype=F32)
    mag = jnp.exp(ldr[..., None] * j)
    pr, pi = mag * jnp.cos(ldi[..., None] * j), mag * jnp.sin(ldi[..., None] * j)
    nr, ni = pr[..., 1] - 1.0, pi[..., 1]
    den = lam_re * lam_re + lam_im * lam_im
    qr, qi = (nr * lam_re + ni * lam_im) / den, (ni * lam_re - nr * lam_im) / den
    bbr = qr[..., None] * b_re - qi[..., None] * b_im
    bbi = qr[..., None] * b_im + qi[..., None] * b_re
    xr = pr[..., :CH, None] * bbr[..., None, :] - pi[..., :CH, None] * bbi[..., None, :]
    xi = pr[..., :CH, None] * bbi[..., None, :] + pi[..., :CH, None] * bbr[..., None, :]
    kk = (jnp.einsum("dghp,dgpji->dgjhi", c_re, xr, precision=HI)
          - jnp.einsum("dghp,dgpji->dgjhi", c_im, xi, precision=HI))
    kf = jnp.transpose(kk[0], (0, 3, 1, 2))
    kb = jnp.transpose(kk[1], (0, 3, 1, 2))
    eye_h = np.eye(SH, dtype=np.float32)
    centre = kf[:, :, 0:1, :] + kb[:, :, 0:1, :] + d.reshape(SG, 1, 1, SH) * eye_h[None, :, None, :]
    lags = jnp.concatenate([kb[:, :, :0:-1, :], centre, kf[:, :, 1:, :], jnp.zeros((SG, SH, 1, SH), F32)], axis=2)
    m = lags.reshape(SG, SH, 2 * CH * SH)
    wfr = jnp.transpose(xr[0][:, :, ::-1, :], (0, 2, 3, 1)).reshape(SG, CH * SH, SP)
    wfi = jnp.transpose(xi[0][:, :, ::-1, :], (0, 2, 3, 1)).reshape(SG, CH * SH, SP)
    wbr = jnp.transpose(xr[1], (0, 2, 3, 1)).reshape(SG, CH * SH, SP)
    wbi = jnp.transpose(xi[1], (0, 2, 3, 1)).reshape(SG, CH * SH, SP)
    ctr, cti = jnp.transpose(c_re, (0, 1, 3, 2)), jnp.transpose(c_im, (0, 1, 3, 2))
    pfr, pfi = pr[0][..., 1:], pi[0][..., 1:]
    pbr, pbi = pr[1][..., 1:][..., ::-1], pi[1][..., 1:][..., ::-1]
    afr = ctr[0][:, :, None, :] * pfr[..., None] - cti[0][:, :, None, :] * pfi[..., None]
    afi = ctr[0][:, :, None, :] * pfi[..., None] + cti[0][:, :, None, :] * pfr[..., None]
    abr = ctr[1][:, :, None, :] * pbr[..., None] - cti[1][:, :, None, :] * pbi[..., None]
    abi = ctr[1][:, :, None, :] * pbi[..., None] + cti[1][:, :, None, :] * pbr[..., None]
    ws = jnp.concatenate([w.astype(BF16) for w in (wfr, wfi, wbr, wbi)], axis=2)
    vs = jnp.concatenate([a.reshape(SG, SP, CH * SH).astype(BF16) for a in (afr, -afi, abr, -abi)], axis=1)
    a16 = jnp.stack([pr[0][..., CH], pi[0][..., CH], pr[1][..., CH], pi[1][..., CH]], 0)
    a16 = a16.reshape(4, NPAIR, 1, LANES)
    return m, ws, vs, a16


TOE_GROUPS = 8


def _toeplitz_kernel(c_ref, o_ref):
    for g in range(TOE_GROUPS):
        cm = c_ref[g]
        for k in range(CH):
            off = (CH - 1 - k) * SH
            o_ref[g, k * SH:(k + 1) * SH, :] = cm[:, off:off + CH * SH].astype(BF16)


def _toeplitz(lags):
    ng = lags.shape[0]
    return pl.pallas_call(
        _toeplitz_kernel,
        grid=(ng // TOE_GROUPS,),
        in_specs=[pl.BlockSpec((TOE_GROUPS, SH, 2 * CH * SH), lambda i: (i, 0, 0))],
        out_specs=pl.BlockSpec((TOE_GROUPS, CH * SH, CH * SH), lambda i: (i, 0, 0)),
        out_shape=jax.ShapeDtypeStruct((ng, CH * SH, CH * SH), BF16),
        compiler_params=_cparams(), name="s5_toeplitz",
    )(lags)


def _ssm_kernel(u_ref, m_ref, ws_ref, vs_ref, a_ref, h0_ref, y_ref, fin_ref, s_ref, ef_ref):
    u0, u1 = u_ref[0], u_ref[1]
    s0 = jnp.dot(u0, ws_ref[0], preferred_element_type=F32)
    s1 = jnp.dot(u1, ws_ref[1], preferred_element_type=F32)
    for k in range(4):
        s_ref[k] = jnp.concatenate([s0[:, k * SP:(k + 1) * SP], s1[:, k * SP:(k + 1) * SP]], axis=1)

    def scan(kr, ki, first, nrows, nsteps, reverse, er, ei):
        ar, ai = a_ref[kr], a_ref[ki]
        order = range(nsteps - 1, -1, -1) if reverse else range(nsteps)
        for c in order:
            rows = pl.ds(first + c, nrows, stride=nsteps)
            ef_ref[kr, rows, :] = er
            ef_ref[ki, rows, :] = ei
            sr, si = s_ref[kr, rows, :], s_ref[ki, rows, :]
            er, ei = ar * er - ai * ei + sr, ar * ei + ai * er + si
        return er, ei

    zc = jnp.zeros((BATCH, LANES), F32)
    for kr, ki, rev in ((0, 1, False), (2, 3, True)):
        er, ei = scan(kr, ki, 0, BATCH, CPS_CTX, rev, zc, zc)
        fin_ref[kr] = er
        fin_ref[ki] = ei
        scan(kr, ki, NCH_CTX, DEC_BATCH, CPS_LAT, rev, h0_ref[kr], h0_ref[ki])

    for q, uq in ((0, u0), (1, u1)):
        ef = jnp.concatenate([ef_ref[k, :, q * SP:(q + 1) * SP] for k in range(4)], axis=1).astype(BF16)
        y_ref[q] = (jnp.dot(uq, m_ref[q], preferred_element_type=F32)
                    + jnp.dot(ef, vs_ref[q], preferred_element_type=F32))


def _ssm(layer, u_chunks, m, ws, vs, a16, h0):
    pair = lambda j: (j, 0, 0)
    return pl.pallas_call(
        _ssm_kernel,
        grid=(NPAIR,),
        in_specs=[pl.BlockSpec((2, NCH, CH * SH), pair),
                  pl.BlockSpec((None, 2, CH * SH, CH * SH), lambda j: (layer, j, 0, 0)),
                  pl.BlockSpec((None, 2, CH * SH, 4 * SP), lambda j: (layer, j, 0, 0)),
                  pl.BlockSpec((None, 2, 4 * SP, CH * SH), lambda j: (layer, j, 0, 0)),
                  pl.BlockSpec((None, 4, None, 1, LANES), lambda j: (layer, 0, j, 0, 0)),
                  pl.BlockSpec((None, 4, None, DEC_BATCH, LANES), lambda j: (layer, 0, j, 0, 0))],
        out_specs=[pl.BlockSpec((2, NCH, CH * SH), pair),
                   pl.BlockSpec((4, None, BATCH, LANES), lambda j: (0, j, 0, 0))],
        out_shape=[jax.ShapeDtypeStruct((SG, NCH, CH * SH), F32),
                   jax.ShapeDtypeStruct((4, NPAIR, BATCH, LANES), F32)],
        scratch_shapes=[pltpu.VMEM((4, NCH, LANES), F32), pltpu.VMEM((4, NCH, LANES), F32)],
        compiler_params=_cparams(), name="s5_chunked",
    )(u_chunks, m, ws, vs, a16, h0)


def _merge_kernel(x_ref, bfc_ref, bfl_ref, bc_ref, ys_ref, gate_ref, mod_ref, wg_ref, bg_ref, wso_ref, wo_ref,
                  n2_ref, rwh_ref, rwl_ref, rb_ref, xo_ref, h_ref, ti_ref, tp_ref, slab_ref):
    i = pl.program_id(0)
    g = jax.nn.gelu(_chunks_to_tokens(ys_ref, slab_ref))
    gl = jnp.dot(g.astype(BF16), wg_ref[...], preferred_element_type=F32) + bg_ref[...]
    y2 = g * _sigmoid(gl)
    brs = jnp.dot(y2.astype(BF16), wso_ref[...], preferred_element_type=F32)
    brf = jnp.where(i < CTX_BLOCKS, bfc_ref[...], bfl_ref[...]).astype(F32)
    mixed = (gate_ref[:, 0:D].astype(F32) * brf
             + gate_ref[:, D:2 * D].astype(F32) * bc_ref[...].astype(F32)
             + gate_ref[:, 2 * D:3 * D].astype(F32) * brs)
    g1 = mod_ref[:, 2 * D:3 * D]
    x = x_ref[...] + g1 * jnp.dot(mixed.astype(BF16), wo_ref[...], preferred_element_type=F32)
    xo_ref[...] = x
    sh2 = mod_ref[:, 3 * D:4 * D]
    sc2 = mod_ref[:, 4 * D:5 * D]
    ms = jnp.mean(x * x, axis=-1, keepdims=True)
    h = (x * lax.rsqrt(ms + EPS)) * n2_ref[...] * (1.0 + sc2) + sh2
    _std_to_rows(h_ref, h, TB)
    hh = h.astype(BF16)
    hl = (h - hh.astype(F32)).astype(BF16)
    logits = (jnp.dot(hh, rwh_ref[...], preferred_element_type=F32)
              + jnp.dot(hl, rwh_ref[...], preferred_element_type=F32)
              + jnp.dot(hh, rwl_ref[...], preferred_element_type=F32)) + rb_ref[...]
    lane = lax.broadcasted_iota(jnp.int32, (TB, LANES), 1).astype(F32)
    vals, idxs = [], []
    l = logits
    for _ in range(TOPK):
        m = jnp.max(l, axis=-1, keepdims=True)
        idx = jnp.min(jnp.where(l == m, lane, float(LANES)), axis=-1, keepdims=True)
        vals.append(m)
        idxs.append(idx)
        l = jnp.where(lane == idx, -jnp.inf, l)
    es = [jnp.exp(v - vals[0]) for v in vals]
    tot = es[0] + es[1] + es[2] + es[3]
    ti = jnp.zeros((TB, LANES), F32)
    tp = jnp.zeros((TB, LANES), F32)
    for k in range(TOPK):
        ti = jnp.where(lane == float(k), idxs[k], ti)
        tp = jnp.where(lane == float(k), es[k] / tot, tp)
    ti_ref[...] = ti.astype(jnp.int32)
    tp_ref[...] = tp


def _merge(layer, x, brf_ctx, brf_lat, brc, ys, gates, mod, w_glu, b_glu, w_ssm_out, w_out, norm2_g,
           rw_hi, rw_lo, rb):
    row = lambda i: (i, 0)
    return pl.pallas_call(
        _merge_kernel,
        grid=(NB,),
        in_specs=[pl.BlockSpec((TB, D), row),
                  pl.BlockSpec((TB, D), lambda i: (jnp.minimum(i, CTX_BLOCKS - 1), 0)),
                  pl.BlockSpec((TB, D), lambda i: (jnp.maximum(i - CTX_BLOCKS, 0), 0)),
                  pl.BlockSpec((TB, D), row),
                  pl.BlockSpec((SG, CPB, CH * SH), lambda i: (0, i, 0)),
                  pl.BlockSpec((TB, 3 * D), row),
                  pl.BlockSpec((None, 1, NMOD * D), lambda i: (_mod_row(i), 0, 0)),
                  _const_spec((SW, SW), layer), _const_spec((1, SW), layer), _const_spec((SW, D), layer),
                  _const_spec((D, D), layer), _const_spec((1, D), layer),
                  _const_spec((D, LANES), layer), _const_spec((D, LANES), layer), _const_spec((1, LANES), layer)],
        out_specs=[pl.BlockSpec((TB, D), row), pl.BlockSpec((TB * DC, LANES), row),
                   pl.BlockSpec((TB, LANES), row), pl.BlockSpec((TB, LANES), row)],
        out_shape=[jax.ShapeDtypeStruct((N, D), F32), jax.ShapeDtypeStruct((N * DC, LANES), F32),
                   jax.ShapeDtypeStruct((N, LANES), jnp.int32), jax.ShapeDtypeStruct((N, LANES), F32)],
        scratch_shapes=[pltpu.VMEM((NM, TB, LANES), F32)],
        compiler_params=_cparams(), name="merge_router",
    )(x, brf_ctx, brf_lat, brc, ys, gates, mod, w_glu, b_glu, w_ssm_out, w_out, norm2_g, rw_hi, rw_lo, rb)


def _routing_tables(top_i, top_p):
    e_flat = top_i[:, :TOPK].reshape(-1)
    p_flat = top_p[:, :TOPK].reshape(-1)
    na = N * TOPK
    key = e_flat * 32768 + jnp.arange(na, dtype=jnp.int32)
    skey, sw = lax.sort_key_val(key, p_flat)
    tok8 = ((skey & 32767) // TOPK) * DC
    tok8 = jnp.concatenate([tok8, jnp.zeros((TM,), jnp.int32)])
    sw = jnp.concatenate([sw, jnp.zeros((TM,), F32)])
    experts = jnp.arange(NE, dtype=jnp.int32)
    counts = jnp.sum((e_flat[:, None] == experts[None, :]).astype(jnp.int32), axis=0)
    ntile = (counts + TM - 1) // TM
    tend = jnp.cumsum(ntile)
    tstart = tend - ntile
    cstart = jnp.cumsum(counts) - counts
    n_tiles = tend[-1]
    later = jnp.logical_and(experts[None, :] > experts[:, None], ntile[None, :] > 0)
    nxt_e = jnp.min(jnp.where(later, experts[None, :], NE), axis=1)
    nxt_e = jnp.where(nxt_e == NE, -1, nxt_e)
    used = (ntile > 0).astype(jnp.int32)
    used_before = jnp.cumsum(used) - used
    ti = jnp.arange(NT_MAX + 1, dtype=jnp.int32)
    t = jnp.minimum(ti, n_tiles - 1)
    tile_e = jnp.minimum(jnp.sum((t[:, None] >= tend[None, :]).astype(jnp.int32), axis=1), NE - 1)
    onehot = (tile_e[:, None] == experts[None, :]).astype(jnp.int32)
    pick = lambda tab: jnp.sum(onehot * tab[None, :], axis=1)
    k = t - pick(tstart)
    tile_j0 = pick(cstart) + k * TM
    tile_cnt = jnp.clip(pick(counts) - k * TM, 0, TM)
    tile_first = jnp.logical_and(k == 0, ti < n_tiles)
    tile_slot = pick(used_before) % 2
    tile_next = pick(nxt_e)
    i32 = lambda a: a.astype(jnp.int32)
    return (tok8, sw, i32(tile_e), i32(tile_j0), i32(tile_cnt), i32(tile_first), i32(tile_slot), i32(tile_next),
            i32(n_tiles.reshape(1)))


GROWS = 16
GTILES = 4
assert NT_MAX % GTILES == 0


def _gather_kernel(tok8_ref, j0_ref, nt_ref, h_ref, o_ref, g_ref):
    for half in range(GTILES):
        t = pl.program_id(0) * GTILES + half
        out_rows = slice(half * TM, (half + 1) * TM)

        @pl.when(t < nt_ref[0])
        def _():
            j0 = j0_ref[t]

            def body(b, carry):
                base = j0 + b * GROWS
                rows = [h_ref[pl.ds(pl.multiple_of(tok8_ref[base + j], DC), DC), :] for j in range(GROWS)]
                for j in range(GROWS):
                    g_ref[pl.ds(pl.multiple_of((b * GROWS + j) * DC, DC), DC), :] = rows[j]
                return carry

            lax.fori_loop(0, TM // GROWS, body, 0)
            for c in range(DC):
                o_ref[out_rows, c * LANES:(c + 1) * LANES] = g_ref[pl.ds(c, TM, stride=DC), :].astype(BF16)

        @pl.when(t >= nt_ref[0])
        def _():
            o_ref[out_rows, :] = jnp.zeros((TM, D), BF16)


def _gather(h_rows, tok8, tile_j0, n_tiles):
    return pl.pallas_call(
        _gather_kernel,
        grid_spec=pltpu.PrefetchScalarGridSpec(
            num_scalar_prefetch=3, grid=(NT_MAX // GTILES,),
            in_specs=[pl.BlockSpec((N * DC, LANES), lambda i, *_: (0, 0))],
            out_specs=pl.BlockSpec((GTILES * TM, D), lambda i, *_: (i, 0)),
            scratch_shapes=[pltpu.VMEM((TM * DC, LANES), F32)]),
        out_shape=jax.ShapeDtypeStruct((R_MAX, D), BF16),
        compiler_params=_cparams(), name="moe_gather",
    )(tok8, tile_j0, n_tiles, h_rows)


def _experts_kernel(layer, tok8_ref, w_ref, te_ref, j0_ref, cnt_ref, first_ref, slot_ref, next_ref, nt_ref,
                    x_ref, wgu_hbm, bgu_ref, wdn_hbm, bdn_ref, acc_ref, y_ref, wgu_buf, wdn_buf, sem):
    i = pl.program_id(0)

    def weight_copies(e, slot):
        return (pltpu.make_async_copy(wgu_hbm.at[layer, e], wgu_buf.at[slot], sem.at[0, slot]),
                pltpu.make_async_copy(wdn_hbm.at[layer, e], wdn_buf.at[slot], sem.at[1, slot]))

    @pl.when(i == 0)
    def _():
        for cp in weight_copies(te_ref[0], 0):
            cp.start()
        acc_ref[...] = jnp.zeros_like(acc_ref)
        y_ref[...] = jnp.zeros_like(y_ref)

    live = i < nt_ref[0]
    slot = slot_ref[i]

    def scatter_previous_tile():
        p = jnp.maximum(i - 1, 0)
        j0 = j0_ref[p]
        cnt = jnp.where(i > 0, cnt_ref[p], 0)
        ys = y_ref.at[(i + 1) % 2]
        for r0 in range(0, TM, SCAT):
            offs, ws = [], []
            for j in range(SCAT):
                real = r0 + j < cnt
                offs.append(pl.multiple_of(jnp.where(real, tok8_ref[j0 + r0 + j], N * DC), DC))
                ws.append(jnp.where(real, w_ref[j0 + r0 + j], 0.0))
            olds = [acc_ref[pl.ds(o, DC), :] for o in offs]
            news = [olds[j] + ws[j] * ys[(r0 + j) * DC:(r0 + j + 1) * DC, :] for j in range(SCAT)]
            for j in range(SCAT):
                acc_ref[pl.ds(offs[j], DC), :] = news[j]

    @pl.when(jnp.logical_and(live, first_ref[i] == 1))
    def _():
        for cp in weight_copies(te_ref[i], slot):
            cp.wait()

        @pl.when(next_ref[i] >= 0)
        def _():
            for cp in weight_copies(next_ref[i], 1 - slot):
                cp.start()

    @pl.when(live)
    def _():
        scatter_previous_tile()
        x = x_ref[...]
        y = jnp.broadcast_to(bdn_ref[...], (TM, D))
        for f in range(0, DFF, FC):
            wg = wgu_buf[slot, :, f:f + FC].astype(BF16)
            wu = wgu_buf[slot, :, DFF + f:DFF + f + FC].astype(BF16)
            gate = jnp.dot(x, wg, preferred_element_type=F32) + bgu_ref[:, f:f + FC]
            up = jnp.dot(x, wu, preferred_element_type=F32) + bgu_ref[:, DFF + f:DFF + f + FC]
            gate = jnp.minimum(gate, LIMIT)
            up = jnp.clip(up, -LIMIT, LIMIT)
            act = (up + 1.0) * (gate * _sigmoid(ALPHA * gate))
            y = y + jnp.dot(act.astype(BF16), wdn_buf[slot, f:f + FC, :].astype(BF16), preferred_element_type=F32)
        _std_to_rows(y_ref.at[i % 2], y, TM)

    @pl.when(i == nt_ref[0])
    def _():
        scatter_previous_tile()


def _experts(layer, x_sorted, tables, w_gate_up, b_gate_up, w_down, b_down):
    tile = lambda i, *s: (jnp.minimum(i, s[-1][0] - 1), 0)
    bias = lambda i, *s: (layer, s[2][i], 0, 0)
    return pl.pallas_call(
        functools.partial(_experts_kernel, layer),
        grid_spec=pltpu.PrefetchScalarGridSpec(
            num_scalar_prefetch=len(tables), grid=(NT_MAX + 1,),
            in_specs=[pl.BlockSpec((TM, D), tile),
                      pl.BlockSpec(memory_space=pl.ANY),
                      pl.BlockSpec((None, None, 1, 2 * DFF), bias),
                      pl.BlockSpec(memory_space=pl.ANY),
                      pl.BlockSpec((None, None, 1, D), bias)],
            out_specs=pl.BlockSpec(((N + SUB) * DC, LANES), lambda i, *_: (0, 0)),
            scratch_shapes=[pltpu.VMEM((2, TM * DC, LANES), F32),
                            pltpu.VMEM((2, D, 2 * DFF), F32), pltpu.VMEM((2, DFF, D), F32),
                            pltpu.SemaphoreType.DMA((2, 2))]),
        out_shape=jax.ShapeDtypeStruct(((N + SUB) * DC, LANES), F32),
        compiler_params=_cparams(VMEM_MOE), name="moe_experts",
    )(*tables, x_sorted, w_gate_up, b_gate_up.reshape(DEPTH, NE, 1, 2 * DFF), w_down,
      b_down.reshape(DEPTH, NE, 1, D))


def _final_kernel(x_ref, moe_ref, mod_ref, g_ref, octx_ref, olat_ref):
    i = pl.program_id(0)
    x = x_ref[...] + mod_ref[:, 5 * D:6 * D] * _rows_to_std(moe_ref, TB)
    ms = jnp.mean(x * x, axis=-1, keepdims=True)
    y = (x * lax.rsqrt(ms + EPS)) * g_ref[...]

    @pl.when(i < CTX_BLOCKS)
    def _():
        octx_ref[...] = y

    @pl.when(i >= CTX_BLOCKS)
    def _():
        olat_ref[...] = y


def _final(x, moe, mod, g):
    row = lambda i: (i, 0)
    return pl.pallas_call(
        _final_kernel,
        grid=(NB,),
        in_specs=[pl.BlockSpec((TB, D), row), pl.BlockSpec((TB * DC, LANES), row),
                  pl.BlockSpec((None, 1, NMOD * D), lambda i: (_mod_row(i), 0, 0)), _const_spec((1, D))],
        out_specs=[pl.BlockSpec((TB, D), lambda i: (jnp.minimum(i, CTX_BLOCKS - 1), 0)),
                   pl.BlockSpec((TB, D), lambda i: (jnp.maximum(i - CTX_BLOCKS, 0), 0))],
        out_shape=[jax.ShapeDtypeStruct((N_CTX, D), F32), jax.ShapeDtypeStruct((N_LAT, D), F32)],
        compiler_params=_cparams(), name="final_norm",
    )(x, moe, mod, g)


def _grid_pos_embed(rows):
    quarter = D // 4
    freqs = jnp.exp(-math.log(10000.0) * jnp.arange(quarter, dtype=F32) / quarter)
    r = jnp.repeat(jnp.arange(rows, dtype=F32), GRID_W)
    col = jnp.tile(jnp.arange(GRID_W, dtype=F32), rows)
    ar = r[:, None] * freqs
    ac = col[:, None] * freqs
    return jnp.concatenate([jnp.sin(ar), jnp.cos(ar), jnp.sin(ac), jnp.cos(ac)], axis=-1)


def kernel(x_prompt, x_sample, state_ssm_re, state_ssm_im, c, c_ctx, w_mod, b_mod, norm1_g, norm2_g, w_in, b_in, w_four, conv_dw, conv_dw_b, conv_ln_g, conv_ln_b, w_conv_out, ssm_lam_re, ssm_lam_im, ssm_log_dt, ssm_b_re, ssm_b_im, ssm_c_re, ssm_c_im, ssm_d, w_ssm_glu, b_ssm_glu, w_ssm_out, w_out, router_w, router_b, w_gate_up, b_gate_up, w_down, b_down, final_norm_g):
    x = (x_prompt.reshape(N_CTX, D), x_sample.reshape(N_LAT, D), _grid_pos_embed(DEC_SEQ // GRID_W))

    cond8 = jnp.concatenate([c_ctx[None, :], c, jnp.zeros((SUB - 1 - DEC_BATCH, D), F32)], axis=0)
    mod_all = _adaln(cond8, w_mod, b_mod)[:, :1 + DEC_BATCH].reshape(DEPTH, 1 + DEC_BATCH, 1, NMOD * D)

    w_in_b, w_four_b, w_conv_b = w_in.astype(BF16), w_four.astype(BF16), w_conv_out.astype(BF16)
    w_glu_b, w_sso_b, w_out_b = w_ssm_glu.astype(BF16), w_ssm_out.astype(BF16), w_out.astype(BF16)
    rw = jnp.pad(router_w, ((0, 0), (0, 0), (0, LANES - NE)))
    rw_hi = rw.astype(BF16)
    rw_lo = (rw - rw_hi.astype(F32)).astype(BF16)
    rb = jnp.pad(router_b, ((0, 0), (0, LANES - NE)), constant_values=-1e30).reshape(DEPTH, 1, LANES)
    r3 = lambda a: a.reshape(DEPTH, 1, a.shape[-1])

    lags_all, ws_all, vs_all, a16_all = jax.vmap(_ssm_params)(
        ssm_lam_re, ssm_lam_im, ssm_log_dt, ssm_b_re, ssm_b_im, ssm_c_re, ssm_c_im, ssm_d)
    m_all = _toeplitz(lags_all.reshape(DEPTH * SG, SH, 2 * CH * SH)).reshape(DEPTH, SG, CH * SH, CH * SH)
    h0_all = jnp.stack([state_ssm_re[:, :, 0], state_ssm_im[:, :, 0], state_ssm_re[:, :, 1], state_ssm_im[:, :, 1]],
                       axis=0)
    h0_all = h0_all.reshape(4, DEC_BATCH, DEPTH, NPAIR, LANES).transpose(2, 0, 3, 1, 4)

    fins = []
    moe = None
    mod_prev = None
    for l in range(DEPTH):
        mod = mod_all[l]
        x, zf, v, u, gates = _inproj(l, x, moe, mod_prev, mod, r3(norm1_g), w_in_b, r3(b_in))
        brf_ctx = _fourier(l, zf, w_four_b, SEQ, 0, BATCH)
        brf_lat = _fourier(l, zf, w_four_b, DEC_SEQ, N_CTX // DEC_SEQ, DEC_BATCH)
        brc = _conv(l, v, conv_dw, r3(conv_dw_b), r3(conv_ln_g), r3(conv_ln_b), w_conv_b)
        ys, fin = _ssm(l, u, m_all, ws_all, vs_all, a16_all, h0_all)
        fins.append(fin)

        x, h_rows, top_i, top_p = _merge(l, x, brf_ctx, brf_lat, brc, ys, gates, mod, w_glu_b, r3(b_ssm_glu),
                                         w_sso_b, w_out_b, r3(norm2_g), rw_hi, rw_lo, rb)
        tables = _routing_tables(top_i, top_p)
        x_sorted = _gather(h_rows, tables[0], tables[3], tables[-1])
        moe = _experts(l, x_sorted, tables, w_gate_up, b_gate_up, w_down, b_down)
        mod_prev = mod

    y_ctx, y_lat = _final(x, moe, mod_prev, final_norm_g.reshape(1, D))
    fin = jnp.stack(fins, axis=0).transpose(3, 0, 1, 2, 4).reshape(BATCH, DEPTH, 2, 2, SG, SP)
    new_re = fin[:, :, :, 0]
    new_im = fin[:, :, :, 1]
    return (y_ctx.reshape(BATCH, SEQ, D), y_lat.reshape(DEC_BATCH, DEC_SEQ, D), new_re, new_im)
```

```python
import functools
import math

import numpy as np
import jax
import jax.numpy as jnp
from jax import lax
from jax.experimental import pallas as pl
from jax.experimental.pallas import tpu as pltpu

F32 = jnp.float32
BF16 = jnp.bfloat16
HI = lax.Precision.HIGHEST

D = 1024
BATCH, SEQ = 16, 256
DEC_BATCH, DEC_SEQ = 2, 1024
DEPTH = 4
N_CTX = BATCH * SEQ
N_LAT = DEC_BATCH * DEC_SEQ
N = N_CTX + N_LAT
GRID_W = 64
FW, FG = 512, 4
FGD = FW // FG
CC, TAPS = 512, 31
PAD = TAPS // 2
SW, SH = 512, 16
SG = SW // SH
SP = 64
IN_COLS = FW + 2 * CC + SW + 3 * D
NE, TOPK, DFF = 32, 4, 1024
LIMIT, ALPHA = 7.0, 1.702
EPS = 1e-6
NMOD = 6

TB = 256
NB = N // TB
CTX_BLOCKS = N_CTX // TB
LAT_BLOCKS_PER_SEQ = DEC_SEQ // TB
CH = 16
NCH = N // CH
NCH_CTX = N_CTX // CH
CPS_CTX = SEQ // CH
CPS_LAT = DEC_SEQ // CH
NPAIR = SG // 2
TM = 256
NT_MAX = (N * TOPK + NE * (TM - 1) + TM - 1) // TM
R_MAX = NT_MAX * TM
SUB = 8
LANES = 128
DC = D // LANES
SCAT = 16
FC = 1024
VMEM_DEFAULT = 48 * 1024 * 1024
VMEM_MOE = 62 * 1024 * 1024


def _cparams(vmem=VMEM_DEFAULT, ndim=1):
    return pltpu.CompilerParams(dimension_semantics=("arbitrary",) * ndim, vmem_limit_bytes=vmem)


def _const_spec(shape, layer=None):
    if layer is None:
        zeros = (0,) * len(shape)
        return pl.BlockSpec(shape, lambda *_: zeros)
    zeros = (0,) * len(shape)
    return pl.BlockSpec((None,) + tuple(shape), lambda *_: (layer,) + zeros, pipeline_mode=pl.Buffered(1))


def _mod_row(i):
    return jnp.where(i < CTX_BLOCKS, 0, 1 + (i - CTX_BLOCKS) // LAT_BLOCKS_PER_SEQ)


def _sigmoid(x):
    return 1.0 / (1.0 + jnp.exp(-x))


def _rows_to_std(ref, rows):
    return jnp.concatenate([ref[pl.ds(c, rows, stride=DC), :] for c in range(DC)], axis=1)


def _std_to_rows(ref, val, rows):
    for c in range(DC):
        ref[pl.ds(c, rows, stride=DC), :] = val[:, c * LANES:(c + 1) * LANES]


NM = SW // LANES
GPV = LANES // SH
CPB = TB // CH


def _lane_group():
    return lax.broadcasted_iota(jnp.int32, (CPB, LANES), 1) // SH


def _tokens_to_chunks(u, slab_ref, out_ref):
    for m in range(NM):
        slab_ref[m] = u[:, m * LANES:(m + 1) * LANES]
    grp = _lane_group()
    for m in range(NM):
        for kk in range(CH // GPV):
            zs = [slab_ref[m, pl.ds(kk * GPV + k8, CPB, stride=CH), :] for k8 in range(GPV)]
            for r in range(GPV):
                acc = None
                for k8 in range(GPV):
                    sh = ((k8 - r) % GPV) * SH
                    piece = zs[k8] if sh == 0 else pltpu.roll(zs[k8], sh, 1)
                    acc = piece if acc is None else jnp.where(grp == k8, piece, acc)
                out_ref[m * GPV + r, :, kk * LANES:(kk + 1) * LANES] = acc.astype(out_ref.dtype)


def _chunks_to_tokens(y_ref, slab_ref):
    grp = _lane_group()
    for m in range(NM):
        for kk in range(CH // GPV):
            ys = [y_ref[m * GPV + r, :, kk * LANES:(kk + 1) * LANES] for r in range(GPV)]
            for t8 in range(GPV):
                acc = None
                for r in range(GPV):
                    sh = ((r - t8) % GPV) * SH
                    piece = ys[r] if sh == 0 else pltpu.roll(ys[r], sh, 1)
                    acc = piece if acc is None else jnp.where(grp == r, piece, acc)
                slab_ref[m, pl.ds(kk * GPV + t8, CPB, stride=CH), :] = acc
    return jnp.concatenate([slab_ref[m] for m in range(NM)], axis=1)


MOD_COLS = 1536


def _mod_kernel(cond_ref, w_ref, b_ref, o_ref):
    c = cond_ref[...]
    s = c * _sigmoid(c)
    o_ref[...] = jnp.dot(s.astype(BF16), w_ref[...].astype(BF16), preferred_element_type=F32) + b_ref[...]


def _adaln(cond8, w_mod, b_mod):
    nc = NMOD * D // MOD_COLS
    return pl.pallas_call(
        _mod_kernel,
        grid=(DEPTH, nc),
        in_specs=[
            pl.BlockSpec((SUB, D), lambda l, j: (0, 0)),
            pl.BlockSpec((None, D, MOD_COLS), lambda l, j: (l, 0, j)),
            pl.BlockSpec((None, 1, MOD_COLS), lambda l, j: (l, 0, j)),
        ],
        out_specs=pl.BlockSpec((None, SUB, MOD_COLS), lambda l, j: (l, 0, j)),
        out_shape=jax.ShapeDtypeStruct((DEPTH, SUB, NMOD * D), F32),
        compiler_params=_cparams(ndim=2),
        name="adaln_mod",
    )(cond8, w_mod, b_mod.reshape(DEPTH, 1, NMOD * D))


def _inproj_kernel(has_moe, *refs):
    if has_moe:
        (x_ref, moe_ref, modp_ref, mod_ref, g_ref, w_ref, b_ref,
         xo_ref, zf_ref, v_ref, u_ref, gate_ref, slab_ref) = refs
        x = x_ref[...] + modp_ref[:, 5 * D:6 * D] * _rows_to_std(moe_ref, TB)
        xo_ref[...] = x
    else:
        (xp_ref, xs_ref, pos_ref, mod_ref, g_ref, w_ref, b_ref,
         xo_ref, zf_ref, v_ref, u_ref, gate_ref, slab_ref) = refs
        x = jnp.where(pl.program_id(0) < CTX_BLOCKS, xp_ref[...], xs_ref[...] + pos_ref[...])
        xo_ref[...] = x
    sh1 = mod_ref[:, 0:D]
    sc1 = mod_ref[:, D:2 * D]
    ms = jnp.mean(x * x, axis=-1, keepdims=True)
    h = (x * lax.rsqrt(ms + EPS)) * g_ref[...] * (1.0 + sc1) + sh1
    z = jnp.dot(h.astype(BF16), w_ref[...], preferred_element_type=F32) + b_ref[...]
    o = 0
    zf_ref[...] = z[:, o:o + FW].astype(BF16)
    o += FW
    za = z[:, o:o + CC]
    zb = z[:, o + CC:o + 2 * CC]
    v_ref[...] = (za * _sigmoid(zb)).astype(BF16)
    o += 2 * CC
    _tokens_to_chunks(z[:, o:o + SW], slab_ref, u_ref)
    o += SW
    gate_ref[...] = _sigmoid(z[:, o:]).astype(BF16)


def _inproj(layer, x, moe, mod_prev, mod, norm_g, w_in, b_in):
    has_moe = moe is not None
    row = lambda i: (i, 0)
    mod_spec = pl.BlockSpec((None, 1, NMOD * D), lambda i: (_mod_row(i), 0, 0))
    if has_moe:
        in_specs = [pl.BlockSpec((TB, D), row), pl.BlockSpec((TB * DC, LANES), row), mod_spec]
        args = [x, moe, mod_prev]
    else:
        lat = lambda i: jnp.maximum(i - CTX_BLOCKS, 0)
        in_specs = [pl.BlockSpec((TB, D), lambda i: (jnp.minimum(i, CTX_BLOCKS - 1), 0)),
                    pl.BlockSpec((TB, D), lambda i: (lat(i), 0)),
                    pl.BlockSpec((TB, D), lambda i: (lat(i) % LAT_BLOCKS_PER_SEQ, 0))]
        args = list(x)
    in_specs += [mod_spec, _const_spec((1, D), layer), _const_spec((D, IN_COLS), layer),
                 _const_spec((1, IN_COLS), layer)]
    args += [mod, norm_g, w_in, b_in]
    out_specs = [pl.BlockSpec((TB, FW), row), pl.BlockSpec((TB, CC), row),
                 pl.BlockSpec((SG, CPB, CH * SH), lambda i: (0, i, 0)), pl.BlockSpec((TB, 3 * D), row)]
    out_shape = [jax.ShapeDtypeStruct((N, FW), BF16), jax.ShapeDtypeStruct((N, CC), BF16),
                 jax.ShapeDtypeStruct((SG, NCH, CH * SH), BF16), jax.ShapeDtypeStruct((N, 3 * D), BF16)]
    out_specs = [pl.BlockSpec((TB, D), row)] + out_specs
    out_shape = [jax.ShapeDtypeStruct((N, D), F32)] + out_shape
    return pl.pallas_call(
        functools.partial(_inproj_kernel, has_moe),
        grid=(NB,), in_specs=in_specs, out_specs=out_specs, out_shape=out_shape,
        scratch_shapes=[pltpu.VMEM((NM, TB, LANES), F32)],
        compiler_params=_cparams(), name=f"inproj_{int(has_moe)}",
    )(*args)


def _fourier_kernel(scale, zf_ref, csbd_ref, csl_ref, wf_ref, o_ref):
    t = jnp.dot(zf_ref[...], csbd_ref[...], preferred_element_type=F32).astype(BF16)
    ts = jnp.concatenate([t[:, :FW], t[:, FW:]], axis=0)
    y = jnp.dot(csl_ref[...], ts, preferred_element_type=F32) * scale
    o_ref[...] = jnp.dot(y.astype(BF16), wf_ref[...], preferred_element_type=F32).astype(BF16)


def _dft_consts(L):
    k = np.arange(L)
    ang = 2.0 * np.pi * ((k[:, None] * k[None, :]) % L) / L
    csl = np.concatenate([np.cos(ang), -np.sin(ang)], axis=1)
    m = np.arange(FGD)
    angc = 2.0 * np.pi * ((m[:, None] * m[None, :]) % FGD) / FGD
    cbd = np.kron(np.eye(FG), np.cos(angc))
    sbd = np.kron(np.eye(FG), np.sin(angc))
    csbd = np.concatenate([cbd, sbd], axis=1)
    return jnp.asarray(csl, F32).astype(BF16), jnp.asarray(csbd, F32).astype(BF16)


def _fourier(layer, zf, w_four, L, first_block, nseq):
    csl, csbd = _dft_consts(L)
    scale = 1.0 / math.sqrt(L * FGD)
    return pl.pallas_call(
        functools.partial(_fourier_kernel, scale),
        grid=(nseq,),
        in_specs=[pl.BlockSpec((L, FW), lambda i: (i + first_block, 0)),
                  _const_spec((FW, 2 * FW)), _const_spec((L, 2 * L)), _const_spec((FW, D), layer)],
        out_specs=pl.BlockSpec((L, D), lambda i: (i, 0)),
        out_shape=jax.ShapeDtypeStruct((nseq * L, D), BF16),
        compiler_params=_cparams(), name=f"fourier_{L}",
    )(zf, csbd, csl, w_four)


HALO = 16
CONV_ROWS = 32


def _conv_kernel(vp_ref, vc_ref, vn_ref, cw_ref, cb_ref, lg_ref, lb_ref, wc_ref, o_ref, pad_ref, act_ref):
    i = pl.program_id(0)
    pos = (i - CTX_BLOCKS) % LAT_BLOCKS_PER_SEQ
    lat = i >= CTX_BLOCKS
    has_prev = jnp.logical_and(lat, pos != 0)
    has_next = jnp.logical_and(lat, pos != LAT_BLOCKS_PER_SEQ - 1)
    prev = vp_ref[TB - HALO:, :].astype(F32)
    nxt = vn_ref[:HALO, :].astype(F32)
    pad_ref[0:HALO, :] = jnp.where(has_prev, prev, 0.0)
    pad_ref[HALO:HALO + TB, :] = vc_ref[...].astype(F32)
    pad_ref[HALO + TB:, :] = jnp.where(has_next, nxt, 0.0)
    for r0 in range(0, TB, CONV_ROWS):
        acc = jnp.broadcast_to(cb_ref[...], (CONV_ROWS, CC))
        for s in range(SUB):
            part = None
            for q in range((TAPS + SUB) // SUB):
                o = SUB * q + s - (HALO - PAD)
                if 0 <= o < TAPS:
                    term = pad_ref[r0 + SUB * q:r0 + SUB * q + CONV_ROWS + SUB, :] * cw_ref[o:o + 1, :]
                    part = term if part is None else part + term
            acc = acc + part[s:s + CONV_ROWS, :]
        mu = jnp.mean(acc, axis=-1, keepdims=True)
        xc = acc - mu
        var = jnp.mean(xc * xc, axis=-1, keepdims=True)
        y = xc * lax.rsqrt(var + EPS) * lg_ref[...] + lb_ref[...]
        act_ref[r0:r0 + CONV_ROWS, :] = (y * _sigmoid(y)).astype(BF16)
    o_ref[...] = jnp.dot(act_ref[...], wc_ref[...], preferred_element_type=F32).astype(BF16)


def _conv(layer, v, conv_dw, conv_b, ln_g, ln_b, w_conv_out):
    return pl.pallas_call(
        _conv_kernel,
        grid=(NB,),
        in_specs=[pl.BlockSpec((TB, CC), lambda i: (jnp.maximum(i - 1, 0), 0)),
                  pl.BlockSpec((TB, CC), lambda i: (i, 0)),
                  pl.BlockSpec((TB, CC), lambda i: (jnp.minimum(i + 1, NB - 1), 0)),
                  _const_spec((TAPS, CC), layer), _const_spec((1, CC), layer), _const_spec((1, CC), layer),
                  _const_spec((1, CC), layer), _const_spec((CC, D), layer)],
        out_specs=pl.BlockSpec((TB, D), lambda i: (i, 0)),
        out_shape=jax.ShapeDtypeStruct((N, D), BF16),
        scratch_shapes=[pltpu.VMEM((TB + 2 * HALO, CC), F32), pltpu.VMEM((TB, CC), BF16)],
        compiler_params=_cparams(), name="conv_branch",
    )(v, v, v, conv_dw, conv_b, ln_g, ln_b, w_conv_out)


def _ssm_params(lam_re, lam_im, log_dt, b_re, b_im, c_re, c_im, d):
    dt = jnp.exp(log_dt)[..., None]
    ldr, ldi = lam_re * dt, lam_im * dt
    j = jnp.arange(CH + 1, dtype=F32)
    mag = jnp.exp(ldr[..., None] * j)
    pr, pi = mag * jnp.cos(ldi[..., None] * j), mag * jnp.sin(ldi[..., None] * j)
    nr, ni = pr[..., 1] - 1.0, pi[..., 1]
    den = lam_re * lam_re + lam_im * lam_im
    qr, qi = (nr * lam_re + ni * lam_im) / den, (ni * lam_re - nr * lam_im) / den
    bbr = qr[..., None] * b_re - qi[..., None] * b_im
    bbi = qr[..., None] * b_im + qi[..., None] * b_re
    xr = pr[..., :CH, None] * bbr[..., None, :] - pi[..., :CH, None] * bbi[..., None, :]
    xi = pr[..., :CH, None] * bbi[..., None, :] + pi[..., :CH, None] * bbr[..., None, :]
    kk = (jnp.einsum("dghp,dgpji->dgjhi", c_re, xr, precision=HI)
          - jnp.einsum("dghp,dgpji->dgjhi", c_im, xi, precision=HI))
    kf = jnp.transpose(kk[0], (0, 3, 1, 2))
    kb = jnp.transpose(kk[1], (0, 3, 1, 2))
    eye_h = np.eye(SH, dtype=np.float32)
    centre = kf[:, :, 0:1, :] + kb[:, :, 0:1, :] + d.reshape(SG, 1, 1, SH) * eye_h[None, :, None, :]
    lags = jnp.concatenate([kb[:, :, :0:-1, :], centre, kf[:, :, 1:, :], jnp.zeros((SG, SH, 1, SH), F32)], axis=2)
    m = lags.reshape(SG, SH, 2 * CH * SH)
    wfr = jnp.transpose(xr[0][:, :, ::-1, :], (0, 2, 3, 1)).reshape(SG, CH * SH, SP)
    wfi = jnp.transpose(xi[0][:, :, ::-1, :], (0, 2, 3, 1)).reshape(SG, CH * SH, SP)
    wbr = jnp.transpose(xr[1], (0, 2, 3, 1)).reshape(SG, CH * SH, SP)
    wbi = jnp.transpose(xi[1], (0, 2, 3, 1)).reshape(SG, CH * SH, SP)
    ctr, cti = jnp.transpose(c_re, (0, 1, 3, 2)), jnp.transpose(c_im, (0, 1, 3, 2))
    pfr, pfi = pr[0][..., 1:], pi[0][..., 1:]
    pbr, pbi = pr[1][..., 1:][..., ::-1], pi[1][..., 1:][..., ::-1]
    afr = ctr[0][:, :, None, :] * pfr[..., None] - cti[0][:, :, None, :] * pfi[..., None]
    afi = ctr[0][:, :, None, :] * pfi[..., None] + cti[0][:, :, None, :] * pfr[..., None]
    abr = ctr[1][:, :, None, :] * pbr[..., None] - cti[1][:, :, None, :] * pbi[..., None]
    abi = ctr[1][:, :, None, :] * pbi[..., None] + cti[1][:, :, None, :] * pbr[..., None]
    ws = jnp.concatenate([w.astype(BF16) for w in (wfr, wfi, wbr, wbi)], axis=2)
    vs = jnp.concatenate([a.reshape(SG, SP, CH * SH).astype(BF16) for a in (afr, -afi, abr, -abi)], axis=1)
    a16 = jnp.stack([pr[0][..., CH], pi[0][..., CH], pr[1][..., CH], pi[1][..., CH]], 0)
    a16 = a16.reshape(4, NPAIR, 1, LANES)
    return m, ws, vs, a16


TOE_GROUPS = 8


def _toeplitz_kernel(c_ref, o_ref):
    for g in range(TOE_GROUPS):
        cm = c_ref[g]
        for k in range(CH):
            off = (CH - 1 - k) * SH
            o_ref[g, k * SH:(k + 1) * SH, :] = cm[:, off:off + CH * SH].astype(BF16)


def _toeplitz(lags):
    ng = lags.shape[0]
    return pl.pallas_call(
        _toeplitz_kernel,
        grid=(ng // TOE_GROUPS,),
        in_specs=[pl.BlockSpec((TOE_GROUPS, SH, 2 * CH * SH), lambda i: (i, 0, 0))],
        out_specs=pl.BlockSpec((TOE_GROUPS, CH * SH, CH * SH), lambda i: (i, 0, 0)),
        out_shape=jax.ShapeDtypeStruct((ng, CH * SH, CH * SH), BF16),
        compiler_params=_cparams(), name="s5_toeplitz",
    )(lags)


def _ssm_kernel(u_ref, m_ref, ws_ref, vs_ref, a_ref, h0_ref, y_ref, fin_ref, s_ref, ef_ref):
    u0, u1 = u_ref[0], u_ref[1]
    s0 = jnp.dot(u0, ws_ref[0], preferred_element_type=F32)
    s1 = jnp.dot(u1, ws_ref[1], preferred_element_type=F32)
    for k in range(4):
        s_ref[k] = jnp.concatenate([s0[:, k * SP:(k + 1) * SP], s1[:, k * SP:(k + 1) * SP]], axis=1)

    def scan(kr, ki, first, nrows, nsteps, reverse, er, ei):
        ar, ai = a_ref[kr], a_ref[ki]
        order = range(nsteps - 1, -1, -1) if reverse else range(nsteps)
        for c in order:
            rows = pl.ds(first + c, nrows, stride=nsteps)
            ef_ref[kr, rows, :] = er
            ef_ref[ki, rows, :] = ei
            sr, si = s_ref[kr, rows, :], s_ref[ki, rows, :]
            er, ei = ar * er - ai * ei + sr, ar * ei + ai * er + si
        return er, ei

    zc = jnp.zeros((BATCH, LANES), F32)
    for kr, ki, rev in ((0, 1, False), (2, 3, True)):
        er, ei = scan(kr, ki, 0, BATCH, CPS_CTX, rev, zc, zc)
        fin_ref[kr] = er
        fin_ref[ki] = ei
        scan(kr, ki, NCH_CTX, DEC_BATCH, CPS_LAT, rev, h0_ref[kr], h0_ref[ki])

    for q, uq in ((0, u0), (1, u1)):
        ef = jnp.concatenate([ef_ref[k, :, q * SP:(q + 1) * SP] for k in range(4)], axis=1).astype(BF16)
        y_ref[q] = (jnp.dot(uq, m_ref[q], preferred_element_type=F32)
                    + jnp.dot(ef, vs_ref[q], preferred_element_type=F32))


def _ssm(layer, u_chunks, m, ws, vs, a16, h0):
    pair = lambda j: (j, 0, 0)
    return pl.pallas_call(
        _ssm_kernel,
        grid=(NPAIR,),
        in_specs=[pl.BlockSpec((2, NCH, CH * SH), pair),
                  pl.BlockSpec((None, 2, CH * SH, CH * SH), lambda j: (layer, j, 0, 0)),
                  pl.BlockSpec((None, 2, CH * SH, 4 * SP), lambda j: (layer, j, 0, 0)),
                  pl.BlockSpec((None, 2, 4 * SP, CH * SH), lambda j: (layer, j, 0, 0)),
                  pl.BlockSpec((None, 4, None, 1, LANES), lambda j: (layer, 0, j, 0, 0)),
                  pl.BlockSpec((None, 4, None, DEC_BATCH, LANES), lambda j: (layer, 0, j, 0, 0))],
        out_specs=[pl.BlockSpec((2, NCH, CH * SH), pair),
                   pl.BlockSpec((4, None, BATCH, LANES), lambda j: (0, j, 0, 0))],
        out_shape=[jax.ShapeDtypeStruct((SG, NCH, CH * SH), F32),
                   jax.ShapeDtypeStruct((4, NPAIR, BATCH, LANES), F32)],
        scratch_shapes=[pltpu.VMEM((4, NCH, LANES), F32), pltpu.VMEM((4, NCH, LANES), F32)],
        compiler_params=_cparams(), name="s5_chunked",
    )(u_chunks, m, ws, vs, a16, h0)


def _merge_kernel(x_ref, bfc_ref, bfl_ref, bc_ref, ys_ref, gate_ref, mod_ref, wg_ref, bg_ref, wso_ref, wo_ref,
                  n2_ref, rwh_ref, rwl_ref, rb_ref, xo_ref, h_ref, ti_ref, tp_ref, slab_ref):
    i = pl.program_id(0)
    g = jax.nn.gelu(_chunks_to_tokens(ys_ref, slab_ref))
    gl = jnp.dot(g.astype(BF16), wg_ref[...], preferred_element_type=F32) + bg_ref[...]
    y2 = g * _sigmoid(gl)
    brs = jnp.dot(y2.astype(BF16), wso_ref[...], preferred_element_type=F32)
    brf = jnp.where(i < CTX_BLOCKS, bfc_ref[...], bfl_ref[...]).astype(F32)
    mixed = (gate_ref[:, 0:D].astype(F32) * brf
             + gate_ref[:, D:2 * D].astype(F32) * bc_ref[...].astype(F32)
             + gate_ref[:, 2 * D:3 * D].astype(F32) * brs)
    g1 = mod_ref[:, 2 * D:3 * D]
    x = x_ref[...] + g1 * jnp.dot(mixed.astype(BF16), wo_ref[...], preferred_element_type=F32)
    xo_ref[...] = x
    sh2 = mod_ref[:, 3 * D:4 * D]
    sc2 = mod_ref[:, 4 * D:5 * D]
    ms = jnp.mean(x * x, axis=-1, keepdims=True)
    h = (x * lax.rsqrt(ms + EPS)) * n2_ref[...] * (1.0 + sc2) + sh2
    _std_to_rows(h_ref, h, TB)
    hh = h.astype(BF16)
    hl = (h - hh.astype(F32)).astype(BF16)
    logits = (jnp.dot(hh, rwh_ref[...], preferred_element_type=F32)
              + jnp.dot(hl, rwh_ref[...], preferred_element_type=F32)
              + jnp.dot(hh, rwl_ref[...], preferred_element_type=F32)) + rb_ref[...]
    lane = lax.broadcasted_iota(jnp.int32, (TB, LANES), 1).astype(F32)
    vals, idxs = [], []
    l = logits
    for _ in range(TOPK):
        m = jnp.max(l, axis=-1, keepdims=True)
        idx = jnp.min(jnp.where(l == m, lane, float(LANES)), axis=-1, keepdims=True)
        vals.append(m)
        idxs.append(idx)
        l = jnp.where(lane == idx, -jnp.inf, l)
    es = [jnp.exp(v - vals[0]) for v in vals]
    tot = es[0] + es[1] + es[2] + es[3]
    ti = jnp.zeros((TB, LANES), F32)
    tp = jnp.zeros((TB, LANES), F32)
    for k in range(TOPK):
        ti = jnp.where(lane == float(k), idxs[k], ti)
        tp = jnp.where(lane == float(k), es[k] / tot, tp)
    ti_ref[...] = ti.astype(jnp.int32)
    tp_ref[...] = tp


def _merge(layer, x, brf_ctx, brf_lat, brc, ys, gates, mod, w_glu, b_glu, w_ssm_out, w_out, norm2_g,
           rw_hi, rw_lo, rb):
    row = lambda i: (i, 0)
    return pl.pallas_call(
        _merge_kernel,
        grid=(NB,),
        in_specs=[pl.BlockSpec((TB, D), row),
                  pl.BlockSpec((TB, D), lambda i: (jnp.minimum(i, CTX_BLOCKS - 1), 0)),
                  pl.BlockSpec((TB, D), lambda i: (jnp.maximum(i - CTX_BLOCKS, 0), 0)),
                  pl.BlockSpec((TB, D), row),
                  pl.BlockSpec((SG, CPB, CH * SH), lambda i: (0, i, 0)),
                  pl.BlockSpec((TB, 3 * D), row),
                  pl.BlockSpec((None, 1, NMOD * D), lambda i: (_mod_row(i), 0, 0)),
                  _const_spec((SW, SW), layer), _const_spec((1, SW), layer), _const_spec((SW, D), layer),
                  _const_spec((D, D), layer), _const_spec((1, D), layer),
                  _const_spec((D, LANES), layer), _const_spec((D, LANES), layer), _const_spec((1, LANES), layer)],
        out_specs=[pl.BlockSpec((TB, D), row), pl.BlockSpec((TB * DC, LANES), row),
                   pl.BlockSpec((TB, LANES), row), pl.BlockSpec((TB, LANES), row)],
        out_shape=[jax.ShapeDtypeStruct((N, D), F32), jax.ShapeDtypeStruct((N * DC, LANES), F32),
                   jax.ShapeDtypeStruct((N, LANES), jnp.int32), jax.ShapeDtypeStruct((N, LANES), F32)],
        scratch_shapes=[pltpu.VMEM((NM, TB, LANES), F32)],
        compiler_params=_cparams(), name="merge_router",
    )(x, brf_ctx, brf_lat, brc, ys, gates, mod, w_glu, b_glu, w_ssm_out, w_out, norm2_g, rw_hi, rw_lo, rb)


def _routing_tables(top_i, top_p):
    e_flat = top_i[:, :TOPK].reshape(-1)
    p_flat = top_p[:, :TOPK].reshape(-1)
    na = N * TOPK
    key = e_flat * 32768 + jnp.arange(na, dtype=jnp.int32)
    skey, sw = lax.sort_key_val(key, p_flat)
    tok8 = ((skey & 32767) // TOPK) * DC
    tok8 = jnp.concatenate([tok8, jnp.zeros((TM,), jnp.int32)])
    sw = jnp.concatenate([sw, jnp.zeros((TM,), F32)])
    experts = jnp.arange(NE, dtype=jnp.int32)
    counts = jnp.sum((e_flat[:, None] == experts[None, :]).astype(jnp.int32), axis=0)
    ntile = (counts + TM - 1) // TM
    tend = jnp.cumsum(ntile)
    tstart = tend - ntile
    cstart = jnp.cumsum(counts) - counts
    n_tiles = tend[-1]
    later = jnp.logical_and(experts[None, :] > experts[:, None], ntile[None, :] > 0)
    nxt_e = jnp.min(jnp.where(later, experts[None, :], NE), axis=1)
    nxt_e = jnp.where(nxt_e == NE, -1, nxt_e)
    used = (ntile > 0).astype(jnp.int32)
    used_before = jnp.cumsum(used) - used
    ti = jnp.arange(NT_MAX + 1, dtype=jnp.int32)
    t = jnp.minimum(ti, n_tiles - 1)
    tile_e = jnp.minimum(jnp.sum((t[:, None] >= tend[None, :]).astype(jnp.int32), axis=1), NE - 1)
    onehot = (tile_e[:, None] == experts[None, :]).astype(jnp.int32)
    pick = lambda tab: jnp.sum(onehot * tab[None, :], axis=1)
    k = t - pick(tstart)
    tile_j0 = pick(cstart) + k * TM
    tile_cnt = jnp.clip(pick(counts) - k * TM, 0, TM)
    tile_first = jnp.logical_and(k == 0, ti < n_tiles)
    tile_slot = pick(used_before) % 2
    tile_next = pick(nxt_e)
    i32 = lambda a: a.astype(jnp.int32)
    return (tok8, sw, i32(tile_e), i32(tile_j0), i32(tile_cnt), i32(tile_first), i32(tile_slot), i32(tile_next),
            i32(n_tiles.reshape(1)))


GROWS = 16
GTILES = 4
assert NT_MAX % GTILES == 0


def _gather_kernel(tok8_ref, j0_ref, nt_ref, h_ref, o_ref, g_ref):
    for half in range(GTILES):
        t = pl.program_id(0) * GTILES + half
        out_rows = slice(half * TM, (half + 1) * TM)

        @pl.when(t < nt_ref[0])
        def _():
            j0 = j0_ref[t]

            def body(b, carry):
                base = j0 + b * GROWS
                rows = [h_ref[pl.ds(pl.multiple_of(tok8_ref[base + j], DC), DC), :] for j in range(GROWS)]
                for j in range(GROWS):
                    g_ref[pl.ds(pl.multiple_of((b * GROWS + j) * DC, DC), DC), :] = rows[j]
                return carry

            lax.fori_loop(0, TM // GROWS, body, 0)
            for c in range(DC):
                o_ref[out_rows, c * LANES:(c + 1) * LANES] = g_ref[pl.ds(c, TM, stride=DC), :].astype(BF16)

        @pl.when(t >= nt_ref[0])
        def _():
            o_ref[out_rows, :] = jnp.zeros((TM, D), BF16)


def _gather(h_rows, tok8, tile_j0, n_tiles):
    return pl.pallas_call(
        _gather_kernel,
        grid_spec=pltpu.PrefetchScalarGridSpec(
            num_scalar_prefetch=3, grid=(NT_MAX // GTILES,),
            in_specs=[pl.BlockSpec((N * DC, LANES), lambda i, *_: (0, 0))],
            out_specs=pl.BlockSpec((GTILES * TM, D), lambda i, *_: (i, 0)),
            scratch_shapes=[pltpu.VMEM((TM * DC, LANES), F32)]),
        out_shape=jax.ShapeDtypeStruct((R_MAX, D), BF16),
        compiler_params=_cparams(), name="moe_gather",
    )(tok8, tile_j0, n_tiles, h_rows)


def _experts_kernel(layer, tok8_ref, w_ref, te_ref, j0_ref, cnt_ref, first_ref, slot_ref, next_ref, nt_ref,
                    x_ref, wgu_hbm, bgu_ref, wdn_hbm, bdn_ref, acc_ref, y_ref, wgu_buf, wdn_buf, sem):
    i = pl.program_id(0)

    def weight_copies(e, slot):
        return (pltpu.make_async_copy(wgu_hbm.at[layer, e], wgu_buf.at[slot], sem.at[0, slot]),
                pltpu.make_async_copy(wdn_hbm.at[layer, e], wdn_buf.at[slot], sem.at[1, slot]))

    @pl.when(i == 0)
    def _():
        for prio, cp in enumerate(weight_copies(te_ref[0], 0)):
            cp.start(priority=prio)
        acc_ref[...] = jnp.zeros_like(acc_ref)
        y_ref[...] = jnp.zeros_like(y_ref)

    live = i < nt_ref[0]
    slot = slot_ref[i]

    def scatter_previous_tile():
        p = jnp.maximum(i - 1, 0)
        j0 = j0_ref[p]
        cnt = jnp.where(i > 0, cnt_ref[p], 0)
        ys = y_ref.at[(i + 1) % 2]
        for r0 in range(0, TM, SCAT):
            offs, ws = [], []
            for j in range(SCAT):
                real = r0 + j < cnt
                offs.append(pl.multiple_of(jnp.where(real, tok8_ref[j0 + r0 + j], N * DC), DC))
                ws.append(jnp.where(real, w_ref[j0 + r0 + j], 0.0))
            olds = [acc_ref[pl.ds(o, DC), :] for o in offs]
            news = [olds[j] + ws[j] * ys[(r0 + j) * DC:(r0 + j + 1) * DC, :] for j in range(SCAT)]
            for j in range(SCAT):
                acc_ref[pl.ds(offs[j], DC), :] = news[j]

    @pl.when(jnp.logical_and(live, first_ref[i] == 1))
    def _():
        for cp in weight_copies(te_ref[i], slot):
            cp.wait()

        @pl.when(next_ref[i] >= 0)
        def _():
            for prio, cp in enumerate(weight_copies(next_ref[i], 1 - slot)):
                cp.start(priority=prio)

    @pl.when(live)
    def _():
        scatter_previous_tile()
        x = x_ref[...]
        y = jnp.broadcast_to(bdn_ref[...], (TM, D))
        for f in range(0, DFF, FC):
            wg = wgu_buf[slot, :, f:f + FC].astype(BF16)
            wu = wgu_buf[slot, :, DFF + f:DFF + f + FC].astype(BF16)
            gate = jnp.dot(x, wg, preferred_element_type=F32) + bgu_ref[:, f:f + FC]
            up = jnp.dot(x, wu, preferred_element_type=F32) + bgu_ref[:, DFF + f:DFF + f + FC]
            gate = jnp.minimum(gate, LIMIT)
            up = jnp.clip(up, -LIMIT, LIMIT)
            act = (up + 1.0) * (gate * _sigmoid(ALPHA * gate))
            y = y + jnp.dot(act.astype(BF16), wdn_buf[slot, f:f + FC, :].astype(BF16), preferred_element_type=F32)
        _std_to_rows(y_ref.at[i % 2], y, TM)

    @pl.when(i == nt_ref[0])
    def _():
        scatter_previous_tile()


def _experts(layer, x_sorted, tables, w_gate_up, b_gate_up, w_down, b_down):
    tile = lambda i, *s: (jnp.minimum(i, s[-1][0] - 1), 0)
    bias = lambda i, *s: (layer, s[2][i], 0, 0)
    return pl.pallas_call(
        functools.partial(_experts_kernel, layer),
        grid_spec=pltpu.PrefetchScalarGridSpec(
            num_scalar_prefetch=len(tables), grid=(NT_MAX + 1,),
            in_specs=[pl.BlockSpec((TM, D), tile),
                      pl.BlockSpec(memory_space=pl.ANY),
                      pl.BlockSpec((None, None, 1, 2 * DFF), bias),
                      pl.BlockSpec(memory_space=pl.ANY),
                      pl.BlockSpec((None, None, 1, D), bias)],
            out_specs=pl.BlockSpec(((N + SUB) * DC, LANES), lambda i, *_: (0, 0)),
            scratch_shapes=[pltpu.VMEM((2, TM * DC, LANES), F32),
                            pltpu.VMEM((2, D, 2 * DFF), F32), pltpu.VMEM((2, DFF, D), F32),
                            pltpu.SemaphoreType.DMA((2, 2))]),
        out_shape=jax.ShapeDtypeStruct(((N + SUB) * DC, LANES), F32),
        compiler_params=_cparams(VMEM_MOE), name="moe_experts",
    )(*tables, x_sorted, w_gate_up, b_gate_up.reshape(DEPTH, NE, 1, 2 * DFF), w_down,
      b_down.reshape(DEPTH, NE, 1, D))


def _final_kernel(x_ref, moe_ref, mod_ref, g_ref, octx_ref, olat_ref):
    i = pl.program_id(0)
    x = x_ref[...] + mod_ref[:, 5 * D:6 * D] * _rows_to_std(moe_ref, TB)
    ms = jnp.mean(x * x, axis=-1, keepdims=True)
    y = (x * lax.rsqrt(ms + EPS)) * g_ref[...]

    @pl.when(i < CTX_BLOCKS)
    def _():
        octx_ref[...] = y

    @pl.when(i >= CTX_BLOCKS)
    def _():
        olat_ref[...] = y


def _final(x, moe, mod, g):
    row = lambda i: (i, 0)
    return pl.pallas_call(
        _final_kernel,
        grid=(NB,),
        in_specs=[pl.BlockSpec((TB, D), row), pl.BlockSpec((TB * DC, LANES), row),
                  pl.BlockSpec((None, 1, NMOD * D), lambda i: (_mod_row(i), 0, 0)), _const_spec((1, D))],
        out_specs=[pl.BlockSpec((TB, D), lambda i: (jnp.minimum(i, CTX_BLOCKS - 1), 0)),
                   pl.BlockSpec((TB, D), lambda i: (jnp.maximum(i - CTX_BLOCKS, 0), 0))],
        out_shape=[jax.ShapeDtypeStruct((N_CTX, D), F32), jax.ShapeDtypeStruct((N_LAT, D), F32)],
        compiler_params=_cparams(), name="final_norm",
    )(x, moe, mod, g)


def _grid_pos_embed(rows):
    quarter = D // 4
    freqs = jnp.exp(-math.log(10000.0) * jnp.arange(quarter, dtype=F32) / quarter)
    r = jnp.repeat(jnp.arange(rows, dtype=F32), GRID_W)
    col = jnp.tile(jnp.arange(GRID_W, dtype=F32), rows)
    ar = r[:, None] * freqs
    ac = col[:, None] * freqs
    return jnp.concatenate([jnp.sin(ar), jnp.cos(ar), jnp.sin(ac), jnp.cos(ac)], axis=-1)


def kernel(x_prompt, x_sample, state_ssm_re, state_ssm_im, c, c_ctx, w_mod, b_mod, norm1_g, norm2_g, w_in, b_in, w_four, conv_dw, conv_dw_b, conv_ln_g, conv_ln_b, w_conv_out, ssm_lam_re, ssm_lam_im, ssm_log_dt, ssm_b_re, ssm_b_im, ssm_c_re, ssm_c_im, ssm_d, w_ssm_glu, b_ssm_glu, w_ssm_out, w_out, router_w, router_b, w_gate_up, b_gate_up, w_down, b_down, final_norm_g):
    x = (x_prompt.reshape(N_CTX, D), x_sample.reshape(N_LAT, D), _grid_pos_embed(DEC_SEQ // GRID_W))

    cond8 = jnp.concatenate([c_ctx[None, :], c, jnp.zeros((SUB - 1 - DEC_BATCH, D), F32)], axis=0)
    mod_all = _adaln(cond8, w_mod, b_mod)[:, :1 + DEC_BATCH].reshape(DEPTH, 1 + DEC_BATCH, 1, NMOD * D)

    w_in_b, w_four_b, w_conv_b = w_in.astype(BF16), w_four.astype(BF16), w_conv_out.astype(BF16)
    w_glu_b, w_sso_b, w_out_b = w_ssm_glu.astype(BF16), w_ssm_out.astype(BF16), w_out.astype(BF16)
    rw = jnp.pad(router_w, ((0, 0), (0, 0), (0, LANES - NE)))
    rw_hi = rw.astype(BF16)
    rw_lo = (rw - rw_hi.astype(F32)).astype(BF16)
    rb = jnp.pad(router_b, ((0, 0), (0, LANES - NE)), constant_values=-1e30).reshape(DEPTH, 1, LANES)
    r3 = lambda a: a.reshape(DEPTH, 1, a.shape[-1])

    lags_all, ws_all, vs_all, a16_all = jax.vmap(_ssm_params)(
        ssm_lam_re, ssm_lam_im, ssm_log_dt, ssm_b_re, ssm_b_im, ssm_c_re, ssm_c_im, ssm_d)
    m_all = _toeplitz(lags_all.reshape(DEPTH * SG, SH, 2 * CH * SH)).reshape(DEPTH, SG, CH * SH, CH * SH)
    h0_all = jnp.stack([state_ssm_re[:, :, 0], state_ssm_im[:, :, 0], state_ssm_re[:, :, 1], state_ssm_im[:, :, 1]],
                       axis=0)
    h0_all = h0_all.reshape(4, DEC_BATCH, DEPTH, NPAIR, LANES).transpose(2, 0, 3, 1, 4)

    fins = []
    moe = None
    mod_prev = None
    for l in range(DEPTH):
        mod = mod_all[l]
        x, zf, v, u, gates = _inproj(l, x, moe, mod_prev, mod, r3(norm1_g), w_in_b, r3(b_in))
        brf_ctx = _fourier(l, zf, w_four_b, SEQ, 0, BATCH)
        brf_lat = _fourier(l, zf, w_four_b, DEC_SEQ, N_CTX // DEC_SEQ, DEC_BATCH)
        brc = _conv(l, v, conv_dw, r3(conv_dw_b), r3(conv_ln_g), r3(conv_ln_b), w_conv_b)
        ys, fin = _ssm(l, u, m_all, ws_all, vs_all, a16_all, h0_all)
        fins.append(fin)

        x, h_rows, top_i, top_p = _merge(l, x, brf_ctx, brf_lat, brc, ys, gates, mod, w_glu_b, r3(b_ssm_glu),
                                         w_sso_b, w_out_b, r3(norm2_g), rw_hi, rw_lo, rb)
        tables = _routing_tables(top_i, top_p)
        x_sorted = _gather(h_rows, tables[0], tables[3], tables[-1])
        moe = _experts(l, x_sorted, tables, w_gate_up, b_gate_up, w_down, b_down)
        mod_prev = mod

    y_ctx, y_lat = _final(x, moe, mod_prev, final_norm_g.reshape(1, D))
    fin = jnp.stack(fins, axis=0).transpose(3, 0, 1, 2, 4).reshape(BATCH, DEPTH, 2, 2, SG, SP)
    new_re = fin[:, :, :, 0]
    new_im = fin[:, :, :, 1]
    return (y_ctx.reshape(BATCH, SEQ, D), y_lat.reshape(DEC_BATCH, DEC_SEQ, D), new_re, new_im)
```
